```python
import jax, jax.numpy as jnp
from jax import lax
import numpy as np

D_MODEL = 1024
BATCH = 32
SEQ = 2048
DEPTH = 4

N_MIXERS = 2
EPS = 1e-6
POOL_WINDOWS = (2, 4, 8, 16)
N_POOL_GROUPS = len(POOL_WINDOWS)
POOL_GROUP = D_MODEL // N_POOL_GROUPS
GLA_HEADS = 4
GLA_KEY_DIM = D_MODEL // 2
GLA_VAL_DIM = D_MODEL
GLA_DK = GLA_KEY_DIM // GLA_HEADS
GLA_DV = GLA_VAL_DIM // GLA_HEADS
GATE_RANK = 16
GATE_NORMALIZER = 16.0
CHUNK = 64
GLA_IN = 2 * GLA_KEY_DIM + 2 * GLA_VAL_DIM + 2 * GATE_RANK
D_FF = -(-8 * D_MODEL // (3 * 256)) * 256
N_POOL_LAYERS = (DEPTH + 1) // 2
N_GLA_LAYERS = DEPTH // 2

kernel_name = "hybrid_pool_gla_encoder"


def rms_norm(x, gain):
    xf = x.astype(jnp.float32)
    y = xf * lax.rsqrt(jnp.mean(xf * xf, axis=-1, keepdims=True) + EPS)
    return (y * gain.astype(jnp.float32)).astype(x.dtype)


def pool_mixer(h, w_group, scale):
    B, S, D = h.shape
    hf = h.astype(jnp.float32)
    csum = jnp.concatenate([jnp.zeros((B, 1, D), jnp.float32), jnp.cumsum(hf, axis=1)], axis=1)
    t = jnp.arange(S)
    outs = []
    for g, win in enumerate(POOL_WINDOWS):
        left = win // 2
        right = win - 1 - left
        lo = jnp.clip(t - left, 0, S)
        hi = jnp.clip(t + right + 1, 0, S)
        cs = csum[:, :, g * POOL_GROUP:(g + 1) * POOL_GROUP]
        cnt = (hi - lo).astype(jnp.float32)[None, :, None]
        mean = (jnp.take(cs, hi, axis=1) - jnp.take(cs, lo, axis=1)) / cnt
        outs.append(mean - hf[:, :, g * POOL_GROUP:(g + 1) * POOL_GROUP])
    mixed = jnp.stack(outs, axis=2).astype(h.dtype)
    y = jnp.einsum('bsgc,gcd->bsgd', mixed, w_group).reshape(B, S, D)
    return y * scale.astype(y.dtype)


def gla_chunked(q, k, v, log_a, strict):
    B, S, H, _ = q.shape
    n = S // CHUNK

    def blocks(a):
        return a.astype(jnp.float32).reshape(B, n, CHUNK, H, -1).transpose(1, 0, 3, 2, 4)

    qb, kb, vb, gb = blocks(q), blocks(k), blocks(v), blocks(log_a)
    b = jnp.cumsum(gb, axis=-2)
    b_last = b[..., -1:, :]
    q_dec = qb * jnp.exp(b)
    k_inv = kb * jnp.exp(-b)
    k_dec = kb * jnp.exp(b_last - b)
    mask = jnp.tril(jnp.ones((CHUNK, CHUNK), bool), k=-1 if strict else 0)
    scores = jnp.where(mask, jnp.einsum('nbhcd,nbhsd->nbhcs', q_dec, k_inv), 0.0)
    o_intra = jnp.einsum('nbhcs,nbhsv->nbhcv', scores, vb)

    def step(state, xs):
        qd, kd, vv, bl = xs
        o = jnp.einsum('bhcd,bhdv->bhcv', qd, state)
        state = jnp.exp(bl)[..., 0, :, None] * state + jnp.einsum('bhcd,bhcv->bhdv', kd, vv)
        return state, o

    s0 = jnp.zeros((B, H, qb.shape[-1], vb.shape[-1]), jnp.float32)
    _, o_inter = lax.scan(step, s0, (q_dec, k_dec, vb, b_last))
    o = o_intra + o_inter
    return o.transpose(1, 0, 3, 2, 4).reshape(B, S, H, vb.shape[-1])


def gla_mixer(h, w_in, w_gate_up, b_gate, head_gain, w_out):
    B, S, D = h.shape
    proj = h @ w_in
    i1 = GLA_KEY_DIM
    i2 = i1 + GLA_KEY_DIM
    i3 = i2 + GLA_VAL_DIM
    i4 = i3 + GLA_VAL_DIM
    i5 = i4 + GATE_RANK
    q = proj[..., :i1].reshape(B, S, GLA_HEADS, GLA_DK) * (GLA_DK ** -0.5)
    k = proj[..., i1:i2].reshape(B, S, GLA_HEADS, GLA_DK)
    v = proj[..., i2:i3].reshape(B, S, GLA_HEADS, GLA_DV)
    r = proj[..., i3:i4].reshape(B, S, GLA_HEADS, GLA_DV)
    g_f = proj[..., i4:i5]
    g_b = proj[..., i5:]

    def log_gate(g_lr, w_up, bias):
        z = (g_lr @ w_up + bias).astype(jnp.float32)
        return (jax.nn.log_sigmoid(z) / GATE_NORMALIZER).reshape(B, S, GLA_HEADS, GLA_DK)

    la_f = log_gate(g_f, w_gate_up[0], b_gate[0])
    la_b = log_gate(g_b, w_gate_up[1], b_gate[1])
    o_fwd = gla_chunked(q, k, v, la_f, strict=False)
    flip = lambda a: jnp.flip(a, axis=1)
    o_bwd = flip(gla_chunked(flip(q), flip(k), flip(v), flip(la_b), strict=True))
    o = (o_fwd + o_bwd).astype(h.dtype)
    o = rms_norm(o, head_gain) * jax.nn.silu(r)
    return o.reshape(B, S, GLA_VAL_DIM) @ w_out


def swiglu(h, w_gate, w_up, w_down):
    return (jax.nn.silu(h @ w_gate) * (h @ w_up)) @ w_down


def setup_inputs(seed: int = 0) -> dict:
    key = jax.random.key(seed)
    ks = jax.random.split(key, 16)
    nrm = jax.random.normal
    f32 = jnp.float32
    return {
        "x": nrm(ks[0], (BATCH, SEQ, D_MODEL), f32),
        "norm_mix": 1.0 + 0.05 * nrm(ks[1], (DEPTH, D_MODEL), f32),
        "norm_ffn": 1.0 + 0.05 * nrm(ks[2], (DEPTH, D_MODEL), f32),
        "norm_final": 1.0 + 0.05 * nrm(ks[3], (D_MODEL,), f32),
        "w_pool": nrm(ks[4], (N_POOL_LAYERS, N_POOL_GROUPS, POOL_GROUP, POOL_GROUP), f32) * POOL_GROUP ** -0.5,
        "pool_scale": 1.0 + 0.05 * nrm(ks[5], (N_POOL_LAYERS, D_MODEL), f32),
        "w_gla_in": nrm(ks[6], (N_GLA_LAYERS, D_MODEL, GLA_IN), f32) * D_MODEL ** -0.5,
        "w_gate_up": nrm(ks[7], (N_GLA_LAYERS, 2, GATE_RANK, GLA_KEY_DIM), f32) * GATE_RANK ** -0.5,
        "b_gate": 0.1 * nrm(ks[8], (N_GLA_LAYERS, 2, GLA_KEY_DIM), f32),
        "gla_head_norm": 1.0 + 0.05 * nrm(ks[9], (N_GLA_LAYERS, GLA_DV), f32),
        "w_gla_out": nrm(ks[10], (N_GLA_LAYERS, GLA_VAL_DIM, D_MODEL), f32) * GLA_VAL_DIM ** -0.5,
        "w_ffn_gate": nrm(ks[11], (DEPTH, D_MODEL, D_FF), f32) * D_MODEL ** -0.5,
        "w_ffn_up": nrm(ks[12], (DEPTH, D_MODEL, D_FF), f32) * D_MODEL ** -0.5,
        "w_ffn_down": nrm(ks[13], (DEPTH, D_FF, D_MODEL), f32) * D_FF ** -0.5,
    }


def reference(x, norm_mix, norm_ffn, norm_final, w_pool, pool_scale, w_gla_in, w_gate_up, b_gate,
              gla_head_norm, w_gla_out, w_ffn_gate, w_ffn_up, w_ffn_down):
    for i in range(DEPTH):
        j = i // N_MIXERS
        h = rms_norm(x, norm_mix[i])
        if i % N_MIXERS == 0:
            x = x + pool_mixer(h, w_pool[j], pool_scale[j])
        else:
            x = x + gla_mixer(h, w_gla_in[j], w_gate_up[j], b_gate[j], gla_head_norm[j], w_gla_out[j])
        h = rms_norm(x, norm_ffn[i])
        x = x + swiglu(h, w_ffn_gate[i], w_ffn_up[i], w_ffn_down[i])
    return rms_norm(x, norm_final)
```

```python
import functools

import jax
import jax.numpy as jnp
from jax import lax
from jax.experimental import pallas as pl
from jax.experimental.pallas import tpu as pltpu

F32 = jnp.float32
BF16 = jnp.bfloat16

EPS = 1e-6
POOL_WINDOWS = (2, 4, 8, 16)
GLA_HEADS = 4
GATE_RANK = 16
GATE_NORMALIZER = 16.0

LANES = 128
SUBLANES = 8
MXU_DIM = 256
VMEM_LIMIT_BYTES = 56 * 1024 * 1024

POOL_HALO = SUBLANES
FF_CHUNK = MXU_DIM
SCAN_CHUNK = 128
TOKEN_TILE = 512


def _rms(x, gain):
    ms = jnp.mean(x * x, axis=-1, keepdims=True)
    return x * lax.rsqrt(ms + EPS) * gain


def _silu(x):
    return x * jax.nn.sigmoid(x)


def _dot(a, b):
    return jnp.dot(a, b, preferred_element_type=F32)


def _dot_nt(a, b):
    return lax.dot_general(a, b, (((1,), (1,)), ((), ())), preferred_element_type=F32)


def _dot_tn(a, b):
    return lax.dot_general(a, b, (((0,), (0,)), ((), ())), preferred_element_type=F32)


def _ffn(x1, gain, wgu_ref, wd_ref, act_ref):
    h = _rms(x1, gain).astype(BF16)
    for c in range(wgu_ref.shape[0]):
        gu = _dot(h, wgu_ref[c])
        g = gu[:, :FF_CHUNK]
        u = gu[:, FF_CHUNK:]
        act_ref[:, c * FF_CHUNK:(c + 1) * FF_CHUNK] = (_silu(g) * u).astype(BF16)
    return x1 + _dot(act_ref[...], wd_ref[...])


def _pool_ffn_kernel(x_ref, xprev_ref, xnext_ref, nmix_ref, wpool_ref, pscale_ref, nffn_ref,
                     wgu_ref, wd_ref, o_ref, hx_ref, x1_ref, act_ref, *, seq_len):
    tm = x_ref.shape[0]
    tiles_per_seq = seq_len // tm
    pos = pl.program_id(0) % tiles_per_seq
    gain = nmix_ref[...]
    x = x_ref[...]
    hx_ref[POOL_HALO:POOL_HALO + tm, :] = _rms(x, gain)
    hx_ref[0:POOL_HALO, :] = jnp.where(pos > 0, _rms(xprev_ref[...], gain), 0.0)
    hx_ref[POOL_HALO + tm:, :] = jnp.where(pos < tiles_per_seq - 1, _rms(xnext_ref[...], gain), 0.0)

    t = pos * tm + lax.broadcasted_iota(jnp.int32, (tm, 1), 0)
    group = x.shape[1] // len(POOL_WINDOWS)
    for g, win in enumerate(POOL_WINDOWS):
        left = win // 2
        right = win - 1 - left
        cols = slice(g * group, (g + 1) * group)
        acc = hx_ref[POOL_HALO - left:POOL_HALO - left + tm, cols]
        for d in range(-left + 1, right + 1):
            acc = acc + hx_ref[POOL_HALO + d:POOL_HALO + d + tm, cols]
        cnt = (jnp.minimum(t + right + 1, seq_len) - jnp.maximum(t - left, 0)).astype(F32)
        mixed = (acc / cnt - hx_ref[POOL_HALO:POOL_HALO + tm, cols]).astype(BF16)
        x1_ref[:, cols] = x[:, cols] + _dot(mixed, wpool_ref[g]) * pscale_ref[:, cols]

    o_ref[...] = _ffn(x1_ref[...], nffn_ref[...], wgu_ref, wd_ref, act_ref)


def _const_spec(shape):
    nd = len(shape)
    return pl.BlockSpec(shape, lambda *_: (0,) * nd, pipeline_mode=pl.Buffered(1))


def _pool_ffn(x2, nmix, wpool, pscale, nffn, wgu, wd, *, seq_len, tm):
    tokens, d = x2.shape
    halo_blocks = tokens // POOL_HALO
    per_tile = tm // POOL_HALO
    d_ff = wd.shape[0]
    return pl.pallas_call(
        functools.partial(_pool_ffn_kernel, seq_len=seq_len),
        grid=(tokens // tm,),
        in_specs=[
            pl.BlockSpec((tm, d), lambda i: (i, 0)),
            pl.BlockSpec((POOL_HALO, d), lambda i: (jnp.maximum(i * per_tile - 1, 0), 0)),
            pl.BlockSpec((POOL_HALO, d), lambda i: (jnp.minimum((i + 1) * per_tile, halo_blocks - 1), 0)),
            _const_spec(nmix.shape),
            _const_spec(wpool.shape),
            _const_spec(pscale.shape),
            _const_spec(nffn.shape),
            _const_spec(wgu.shape),
            _const_spec(wd.shape),
        ],
        out_specs=pl.BlockSpec((tm, d), lambda i: (i, 0)),
        out_shape=jax.ShapeDtypeStruct((tokens, d), F32),
        scratch_shapes=[
            pltpu.VMEM((tm + 2 * POOL_HALO, d), F32),
            pltpu.VMEM((tm, d), F32),
            pltpu.VMEM((tm, d_ff), BF16),
        ],
        compiler_params=pltpu.CompilerParams(
            dimension_semantics=("arbitrary",), vmem_limit_bytes=VMEM_LIMIT_BYTES),
        name="pool_ffn",
    )(x2, x2, x2, nmix, wpool, pscale, nffn, wgu, wd)


def _prefix_rows(x):
    n = x.shape[0]
    row = lax.broadcasted_iota(jnp.int32, x.shape, 0)
    s = 1
    while s < n:
        x = x + jnp.where(row >= s, pltpu.roll(x, s, axis=0), 0.0)
        s *= 2
    return x


def _suffix_rows(x):
    n = x.shape[0]
    row = lax.broadcasted_iota(jnp.int32, x.shape, 0)
    s = 1
    while s < n:
        x = x + jnp.where(row < n - s, pltpu.roll(x, n - s, axis=0), 0.0)
        s *= 2
    return x


def _log_sigmoid(z):
    return jnp.minimum(z, 0.0) - jnp.log1p(jnp.exp(-jnp.abs(z)))


def _gla_in_kernel(x_ref, nmix_ref, wq_ref, wk_ref, wv_ref, wr_ref, wg_ref, wup_ref, bg_ref,
                   qf_ref, kf_ref, qb_ref, kb_ref, v_ref, r_ref, cv_ref, *, q_scale):
    tm = x_ref.shape[0]
    dk = wq_ref.shape[1]
    h = _rms(x_ref[...], nmix_ref[...]).astype(BF16)
    v_ref[...] = _dot(h, wv_ref[...]).astype(BF16)
    r_ref[...] = _dot(h, wr_ref[...])
    q = _dot(h, wq_ref[...]) * q_scale
    k = _dot(h, wk_ref[...])
    g_lr = _dot(h, wg_ref[...]).astype(BF16)
    log_a = _log_sigmoid(_dot(g_lr, wup_ref[...]) + bg_ref[...]) / GATE_NORMALIZER

    c = SCAN_CHUNK
    mid = c // 2
    vec_row = lax.broadcasted_iota(jnp.int32, (SUBLANES, dk), 0)
    for ci in range(tm // c):
        rows = slice(ci * c, (ci + 1) * c)
        qc = q[rows]
        kc = k[rows]
        b = _prefix_rows(log_a[rows, :dk])
        b_mid = b[mid - 1:mid]
        b_tot = b[c - 1:c]
        qf_ref[rows, :] = (qc * jnp.exp(b - b_mid)).astype(BF16)
        kf_ref[rows, :] = (kc * jnp.exp(b_mid - b)).astype(BF16)
        s = _suffix_rows(log_a[rows, dk:])
        s_mid = s[mid:mid + 1]
        s_tot = s[0:1]
        qb_ref[rows, :] = (qc * jnp.exp(s - s_mid)).astype(BF16)
        kb_ref[rows, :] = (kc * jnp.exp(s_mid - s)).astype(BF16)
        vecs = (jnp.exp(b_mid), jnp.exp(b_tot - b_mid), jnp.exp(b_tot),
                jnp.exp(s_mid), jnp.exp(s_tot - s_mid), jnp.exp(s_tot))
        cv = jnp.zeros((SUBLANES, dk), F32)
        for j, vec in enumerate(vecs):
            cv = jnp.where(vec_row == j, vec, cv)
        cv_ref[ci] = cv


def _gla_in(x2, nmix, wq, wk, wv, wr, wg, wup, bg, *, q_scale, tm):
    tokens, d = x2.shape
    dk = wq.shape[1]
    dv = wv.shape[1]
    n_chunks = tokens // SCAN_CHUNK
    tile = lambda n: pl.BlockSpec((tm, n), lambda i: (i, 0))
    return pl.pallas_call(
        functools.partial(_gla_in_kernel, q_scale=q_scale),
        grid=(tokens // tm,),
        in_specs=[tile(d)] + [_const_spec(a.shape) for a in (nmix, wq, wk, wv, wr, wg, wup, bg)],
        out_specs=[tile(dk), tile(dk), tile(dk), tile(dk), tile(dv), tile(dv),
                   pl.BlockSpec((tm // SCAN_CHUNK, SUBLANES, dk), lambda i: (i, 0, 0))],
        out_shape=[jax.ShapeDtypeStruct((tokens, dk), BF16)] * 4
        + [jax.ShapeDtypeStruct((tokens, dv), BF16), jax.ShapeDtypeStruct((tokens, dv), F32),
           jax.ShapeDtypeStruct((n_chunks, SUBLANES, dk), F32)],
        compiler_params=pltpu.CompilerParams(
            dimension_semantics=("arbitrary",), vmem_limit_bytes=VMEM_LIMIT_BYTES),
        name="gla_in",
    )(x2, nmix, wq, wk, wv, wr, wg, wup, bg)


def _gla_scan_kernel(qf_ref, kf_ref, qb_ref, kb_ref, v_ref, cv_ref, o_ref, ut_ref, st_ref, snap_ref):
    c = SCAN_CHUNK
    n = qf_ref.shape[0] // c
    dk = qf_ref.shape[1]

    def rows_of(i):
        return pl.ds(pl.multiple_of(i * c, c), c)

    def increments(i, carry):
        rows = rows_of(i)
        kk = jnp.concatenate([kf_ref[rows, :], kb_ref[rows, :]], axis=1)
        ut_ref[i] = _dot_tn(v_ref[rows, :], kk)
        return carry

    lax.fori_loop(0, n, increments, 0)

    st_ref[...] = jnp.zeros_like(st_ref)

    def recur(j, carry):
        jb = n - 1 - j
        cvf = cv_ref[j]
        cvb = cv_ref[jb]
        stf = st_ref[:, :dk]
        stb = st_ref[:, dk:]
        snap_ref[j, :, :dk] = (stf * cvf[0:1]).astype(BF16)
        snap_ref[jb, :, dk:] = (stb * cvb[3:4]).astype(BF16)
        st_ref[:, :dk] = stf * cvf[2:3] + ut_ref[j, :, :dk] * cvf[1:2]
        st_ref[:, dk:] = stb * cvb[5:6] + ut_ref[jb, :, dk:] * cvb[4:5]
        return carry

    lax.fori_loop(0, n, recur, 0)

    qi = lax.broadcasted_iota(jnp.int32, (c, c), 0)
    ki = lax.broadcasted_iota(jnp.int32, (c, c), 1)
    fwd_mask = ki <= qi

    def outputs(i, carry):
        rows = rows_of(i)
        qf = qf_ref[rows, :]
        qb = qb_ref[rows, :]
        scores = _dot_nt(jnp.concatenate([qf, qb], axis=0),
                         jnp.concatenate([kf_ref[rows, :], kb_ref[rows, :]], axis=0))
        p = jnp.where(fwd_mask, scores[:c, :c], scores[c:, c:]).astype(BF16)
        o_ref[rows, :] = (_dot(p, v_ref[rows, :])
                          + _dot_nt(jnp.concatenate([qf, qb], axis=1), snap_ref[i]))
        return carry

    lax.fori_loop(0, n, outputs, 0)


def _gla_scan(qf, kf, qb, kb, v, cv, *, batch, seq_len):
    tokens, dk_all = qf.shape
    dv_all = v.shape[1]
    dk = dk_all // GLA_HEADS
    dv = dv_all // GLA_HEADS
    n = seq_len // SCAN_CHUNK
    qk_spec = pl.BlockSpec((seq_len, dk), lambda b, h: (b, h))
    return pl.pallas_call(
        _gla_scan_kernel,
        grid=(batch, GLA_HEADS),
        in_specs=[qk_spec, qk_spec, qk_spec, qk_spec,
                  pl.BlockSpec((seq_len, dv), lambda b, h: (b, h)),
                  pl.BlockSpec((n, SUBLANES, dk), lambda b, h: (b, 0, h))],
        out_specs=pl.BlockSpec((seq_len, dv), lambda b, h: (b, h)),
        out_shape=jax.ShapeDtypeStruct((tokens, dv_all), F32),
        scratch_shapes=[
            pltpu.VMEM((n, dv, 2 * dk), F32),
            pltpu.VMEM((dv, 2 * dk), F32),
            pltpu.VMEM((n, dv, 2 * dk), BF16),
        ],
        compiler_params=pltpu.CompilerParams(
            dimension_semantics=("arbitrary", "arbitrary"), vmem_limit_bytes=VMEM_LIMIT_BYTES),
        name="gla_scan",
    )(qf, kf, qb, kb, v, cv)


def _gla_out_ffn_kernel(x_ref, o_ref, r_ref, hgain_ref, wout_ref, nffn_ref, wgu_ref, wd_ref,
                        nfin_ref, out_ref, act_ref, *, final):
    dv = hgain_ref.shape[1]
    gated = []
    for hh in range(o_ref.shape[1] // dv):
        cols = slice(hh * dv, (hh + 1) * dv)
        gated.append((_rms(o_ref[:, cols], hgain_ref[...]) * _silu(r_ref[:, cols])).astype(BF16))
    x1 = x_ref[...] + _dot(jnp.concatenate(gated, axis=1), wout_ref[...])
    y = _ffn(x1, nffn_ref[...], wgu_ref, wd_ref, act_ref)
    out_ref[...] = _rms(y, nfin_ref[...]) if final else y


def _gla_out_ffn(x2, o, r, hgain, wout, nffn, wgu, wd, nfin, *, final, tm):
    tokens, d = x2.shape
    d_ff = wd.shape[0]
    tile = lambda n: pl.BlockSpec((tm, n), lambda i: (i, 0))
    return pl.pallas_call(
        functools.partial(_gla_out_ffn_kernel, final=final),
        grid=(tokens // tm,),
        in_specs=[tile(d), tile(o.shape[1]), tile(r.shape[1])]
        + [_const_spec(a.shape) for a in (hgain, wout, nffn, wgu, wd, nfin)],
        out_specs=tile(d),
        out_shape=jax.ShapeDtypeStruct((tokens, d), F32),
        scratch_shapes=[pltpu.VMEM((tm, d_ff), BF16)],
        compiler_params=pltpu.CompilerParams(
            dimension_semantics=("arbitrary",), vmem_limit_bytes=VMEM_LIMIT_BYTES),
        name="gla_out_ffn",
    )(x2, o, r, hgain, wout, nffn, wgu, wd, nfin)


def _ffn_weights(w_gate, w_up, w_down):
    d, d_ff = w_gate.shape
    n = d_ff // FF_CHUNK
    wgu = jnp.concatenate([w_gate.reshape(d, n, FF_CHUNK), w_up.reshape(d, n, FF_CHUNK)], axis=-1)
    return wgu.transpose(1, 0, 2).astype(BF16), w_down.astype(BF16)


def kernel(x, norm_mix, norm_ffn, norm_final, w_pool, pool_scale, w_gla_in, w_gate_up, b_gate,
           gla_head_norm, w_gla_out, w_ffn_gate, w_ffn_up, w_ffn_down):
    batch, seq_len, d = x.shape
    depth = norm_mix.shape[0]
    key_dim = w_gate_up.shape[-1]
    val_dim = w_gla_out.shape[1]
    dk = key_dim // GLA_HEADS
    assert depth % 2 == 0, "the final norm is fused into the last GLA layer"
    assert seq_len % TOKEN_TILE == 0 and TOKEN_TILE % SCAN_CHUNK == 0
    assert d % (len(POOL_WINDOWS) * LANES) == 0 and dk % LANES == 0
    row = lambda a: a.reshape(1, -1)

    x2 = x.reshape(batch * seq_len, d)
    for i in range(depth):
        j = i // 2
        wgu, wd = _ffn_weights(w_ffn_gate[i], w_ffn_up[i], w_ffn_down[i])
        if i % 2 == 0:
            x2 = _pool_ffn(x2, row(norm_mix[i]), w_pool[j].astype(BF16), row(pool_scale[j]),
                           row(norm_ffn[i]), wgu, wd, seq_len=seq_len, tm=TOKEN_TILE)
        else:
            w_in = w_gla_in[j]
            i1 = key_dim
            i2 = i1 + key_dim
            i3 = i2 + val_dim
            i4 = i3 + val_dim
            wg = jnp.zeros((d, LANES), F32).at[:, :2 * GATE_RANK].set(w_in[:, i4:])
            wup = jnp.zeros((LANES, 2 * key_dim), F32)
            wup = wup.at[:GATE_RANK, :key_dim].set(w_gate_up[j, 0])
            wup = wup.at[GATE_RANK:2 * GATE_RANK, key_dim:].set(w_gate_up[j, 1])
            qf, kf, qb, kb, v, r, cv = _gla_in(
                x2, row(norm_mix[i]), w_in[:, :i1].astype(BF16), w_in[:, i1:i2].astype(BF16),
                w_in[:, i2:i3].astype(BF16), w_in[:, i3:i4].astype(BF16), wg.astype(BF16),
                wup.astype(BF16), b_gate[j].reshape(1, -1), q_scale=dk ** -0.5, tm=TOKEN_TILE)
            o = _gla_scan(qf, kf, qb, kb, v, cv, batch=batch, seq_len=seq_len)
            x2 = _gla_out_ffn(x2, o, r, row(gla_head_norm[j]), w_gla_out[j].astype(BF16),
                              row(norm_ffn[i]), wgu, wd, row(norm_final),
                              final=(i == depth - 1), tm=TOKEN_TILE)
    return x2.reshape(batch, seq_len, d)
```

```python
import functools

import jax
import jax.numpy as jnp
from jax import lax
from jax.experimental import pallas as pl
from jax.experimental.pallas import tpu as pltpu

F32 = jnp.float32
BF16 = jnp.bfloat16

EPS = 1e-6
POOL_WINDOWS = (2, 4, 8, 16)
GLA_HEADS = 4
GATE_RANK = 16
GATE_NORMALIZER = 16.0

LANES = 128
SUBLANES = 8
MXU_DIM = 256
VMEM_LIMIT_BYTES = 56 * 1024 * 1024

POOL_HALO = SUBLANES
FF_CHUNK = MXU_DIM
SCAN_CHUNK = 128
TOKEN_TILE = 512


def _rms(x, gain):
    ms = jnp.mean(x * x, axis=-1, keepdims=True)
    return x * lax.rsqrt(ms + EPS) * gain


def _silu(x):
    return x * jax.nn.sigmoid(x)


def _dot(a, b):
    return jnp.dot(a, b, preferred_element_type=F32)


def _dot_nt(a, b):
    return lax.dot_general(a, b, (((1,), (1,)), ((), ())), preferred_element_type=F32)


def _dot_tn(a, b):
    return lax.dot_general(a, b, (((0,), (0,)), ((), ())), preferred_element_type=F32)


def _ffn(x1, gain, wgu_ref, wd_ref, act_ref):
    h = _rms(x1, gain).astype(BF16)
    for c in range(wgu_ref.shape[0]):
        gu = _dot(h, wgu_ref[c])
        g = gu[:, :FF_CHUNK]
        u = gu[:, FF_CHUNK:]
        act_ref[:, c * FF_CHUNK:(c + 1) * FF_CHUNK] = (_silu(g) * u).astype(BF16)
    return x1 + _dot(act_ref[...], wd_ref[...])


def _window_sum(s, win, tm):
    n = s.shape[0]
    assert win & (win - 1) == 0 and win <= 2 * POOL_HALO

    def ahead(a, k):
        return a if k == 0 else pltpu.roll(a, n - k, axis=0)

    width = 1
    while 2 * width < win:
        s = s + ahead(s, width)
        width *= 2
    start = POOL_HALO - win // 2
    return (ahead(s, start) + ahead(s, start + width))[:tm]


def _pool_ffn_kernel(x_ref, xprev_ref, xnext_ref, nmix_ref, wpool_ref, pscale_ref, nffn_ref,
                     wgu_ref, wd_ref, o_ref, hx_ref, x1_ref, act_ref, *, seq_len):
    tm = x_ref.shape[0]
    tiles_per_seq = seq_len // tm
    pos = pl.program_id(0) % tiles_per_seq
    gain = nmix_ref[...]
    x = x_ref[...]
    hx_ref[POOL_HALO:POOL_HALO + tm, :] = _rms(x, gain)
    hx_ref[0:POOL_HALO, :] = jnp.where(pos > 0, _rms(xprev_ref[...], gain), 0.0)
    hx_ref[POOL_HALO + tm:, :] = jnp.where(pos < tiles_per_seq - 1, _rms(xnext_ref[...], gain), 0.0)

    t = pos * tm + lax.broadcasted_iota(jnp.int32, (tm, 1), 0)
    group = x.shape[1] // len(POOL_WINDOWS)
    for g, win in enumerate(POOL_WINDOWS):
        left = win // 2
        right = win - 1 - left
        cols = slice(g * group, (g + 1) * group)
        acc = _window_sum(hx_ref[:, cols], win, tm)
        cnt = (jnp.minimum(t + right + 1, seq_len) - jnp.maximum(t - left, 0)).astype(F32)
        mixed = (acc / cnt - hx_ref[POOL_HALO:POOL_HALO + tm, cols]).astype(BF16)
        x1_ref[:, cols] = x[:, cols] + _dot(mixed, wpool_ref[g]) * pscale_ref[:, cols]

    o_ref[...] = _ffn(x1_ref[...], nffn_ref[...], wgu_ref, wd_ref, act_ref)


def _const_spec(shape):
    nd = len(shape)
    return pl.BlockSpec(shape, lambda *_: (0,) * nd, pipeline_mode=pl.Buffered(1))


def _pool_ffn(x2, nmix, wpool, pscale, nffn, wgu, wd, *, seq_len, tm):
    tokens, d = x2.shape
    halo_blocks = tokens // POOL_HALO
    per_tile = tm // POOL_HALO
    d_ff = wd.shape[0]
    return pl.pallas_call(
        functools.partial(_pool_ffn_kernel, seq_len=seq_len),
        grid=(tokens // tm,),
        in_specs=[
            pl.BlockSpec((tm, d), lambda i: (i, 0)),
            pl.BlockSpec((POOL_HALO, d), lambda i: (jnp.maximum(i * per_tile - 1, 0), 0)),
            pl.BlockSpec((POOL_HALO, d), lambda i: (jnp.minimum((i + 1) * per_tile, halo_blocks - 1), 0)),
            _const_spec(nmix.shape),
            _const_spec(wpool.shape),
            _const_spec(pscale.shape),
            _const_spec(nffn.shape),
            _const_spec(wgu.shape),
            _const_spec(wd.shape),
        ],
        out_specs=pl.BlockSpec((tm, d), lambda i: (i, 0)),
        out_shape=jax.ShapeDtypeStruct((tokens, d), F32),
        scratch_shapes=[
            pltpu.VMEM((tm + 2 * POOL_HALO, d), F32),
            pltpu.VMEM((tm, d), F32),
            pltpu.VMEM((tm, d_ff), BF16),
        ],
        compiler_params=pltpu.CompilerParams(
            dimension_semantics=("arbitrary",), vmem_limit_bytes=VMEM_LIMIT_BYTES),
        name="pool_ffn",
    )(x2, x2, x2, nmix, wpool, pscale, nffn, wgu, wd)


def _centred_sum_matrices(c):
    r = lax.broadcasted_iota(jnp.int32, (c, 2 * c), 0)
    j = lax.broadcasted_iota(jnp.int32, (c, 2 * c), 1) & (c - 1)
    one = lambda m: jnp.where(m, 1.0, 0.0)
    fwd = one(j <= r) - one(j < c // 2)
    bwd = one(j >= r) - one(j >= c // 2)
    return fwd.astype(BF16), bwd.astype(BF16)


def _split_bf16(x):
    hi = x.astype(BF16)
    lo = (x - hi.astype(F32)).astype(BF16)
    return jnp.concatenate([hi, lo], axis=0)


def _log_sigmoid(z):
    return jnp.minimum(z, 0.0) - jnp.log1p(jnp.exp(-jnp.abs(z)))


def _gla_in_kernel(x_ref, nmix_ref, wq_ref, wk_ref, wv_ref, wr_ref, wg_ref, wup_ref, bg_ref,
                   qf_ref, kf_ref, qb_ref, kb_ref, v_ref, r_ref, cv_ref, *, q_scale):
    tm = x_ref.shape[0]
    dk = wq_ref.shape[1]
    h = _rms(x_ref[...], nmix_ref[...]).astype(BF16)
    v_ref[...] = _dot(h, wv_ref[...]).astype(BF16)
    r_ref[...] = _dot(h, wr_ref[...])
    q = _dot(h, wq_ref[...]) * q_scale
    k = _dot(h, wk_ref[...])
    g_lr = _dot(h, wg_ref[...]).astype(BF16)
    log_a = _log_sigmoid(_dot(g_lr, wup_ref[...]) + bg_ref[...]) / GATE_NORMALIZER

    c = SCAN_CHUNK
    sum_fwd, sum_bwd = _centred_sum_matrices(c)
    vec_row = lax.broadcasted_iota(jnp.int32, (SUBLANES, dk), 0)
    for ci in range(tm // c):
        rows = slice(ci * c, (ci + 1) * c)
        qc = q[rows]
        kc = k[rows]
        la_f = log_a[rows, :dk]
        cf = _dot(sum_fwd, _split_bf16(la_f))
        b_mid = la_f[0:1] - cf[0:1]
        qf_ref[rows, :] = (qc * jnp.exp(cf)).astype(BF16)
        kf_ref[rows, :] = (kc * jnp.exp(-cf)).astype(BF16)
        la_b = log_a[rows, dk:]
        cb = _dot(sum_bwd, _split_bf16(la_b))
        s_mid = la_b[c - 1:c] - cb[c - 1:c]
        qb_ref[rows, :] = (qc * jnp.exp(cb)).astype(BF16)
        kb_ref[rows, :] = (kc * jnp.exp(-cb)).astype(BF16)
        vecs = (jnp.exp(b_mid), jnp.exp(cf[c - 1:c]), jnp.exp(b_mid + cf[c - 1:c]),
                jnp.exp(s_mid), jnp.exp(cb[0:1]), jnp.exp(s_mid + cb[0:1]))
        cv = jnp.zeros((SUBLANES, dk), F32)
        for j, vec in enumerate(vecs):
            cv = jnp.where(vec_row == j, vec, cv)
        cv_ref[ci] = cv


def _gla_in(x2, nmix, wq, wk, wv, wr, wg, wup, bg, *, q_scale, tm):
    tokens, d = x2.shape
    dk = wq.shape[1]
    dv = wv.shape[1]
    n_chunks = tokens // SCAN_CHUNK
    tile = lambda n: pl.BlockSpec((tm, n), lambda i: (i, 0))
    return pl.pallas_call(
        functools.partial(_gla_in_kernel, q_scale=q_scale),
        grid=(tokens // tm,),
        in_specs=[tile(d)] + [_const_spec(a.shape) for a in (nmix, wq, wk, wv, wr, wg, wup, bg)],
        out_specs=[tile(dk), tile(dk), tile(dk), tile(dk), tile(dv), tile(dv),
                   pl.BlockSpec((tm // SCAN_CHUNK, SUBLANES, dk), lambda i: (i, 0, 0))],
        out_shape=[jax.ShapeDtypeStruct((tokens, dk), BF16)] * 4
        + [jax.ShapeDtypeStruct((tokens, dv), BF16), jax.ShapeDtypeStruct((tokens, dv), F32),
           jax.ShapeDtypeStruct((n_chunks, SUBLANES, dk), F32)],
        compiler_params=pltpu.CompilerParams(
            dimension_semantics=("arbitrary",), vmem_limit_bytes=VMEM_LIMIT_BYTES),
        name="gla_in",
    )(x2, nmix, wq, wk, wv, wr, wg, wup, bg)


def _gla_scan_kernel(qf_ref, kf_ref, qb_ref, kb_ref, v_ref, cv_ref, o_ref, ut_ref, st_ref, snap_ref):
    c = SCAN_CHUNK
    n = qf_ref.shape[0] // c
    dk = qf_ref.shape[1]

    def rows_of(i):
        return pl.ds(pl.multiple_of(i * c, c), c)

    def increments(i, carry):
        rows = rows_of(i)
        kk = jnp.concatenate([kf_ref[rows, :], kb_ref[rows, :]], axis=1)
        ut_ref[i] = _dot_tn(v_ref[rows, :], kk)
        return carry

    lax.fori_loop(0, n, increments, 0, unroll=True)

    st_ref[...] = jnp.zeros_like(st_ref)

    def recur(j, carry):
        jb = n - 1 - j
        cvf = cv_ref[j]
        cvb = cv_ref[jb]
        stf = st_ref[:, :dk]
        stb = st_ref[:, dk:]
        snap_ref[j, :, :dk] = (stf * cvf[0:1]).astype(BF16)
        snap_ref[jb, :, dk:] = (stb * cvb[3:4]).astype(BF16)
        st_ref[:, :dk] = stf * cvf[2:3] + ut_ref[j, :, :dk] * cvf[1:2]
        st_ref[:, dk:] = stb * cvb[5:6] + ut_ref[jb, :, dk:] * cvb[4:5]
        return carry

    lax.fori_loop(0, n, recur, 0, unroll=True)

    qi = lax.broadcasted_iota(jnp.int32, (c, c), 0)
    ki = lax.broadcasted_iota(jnp.int32, (c, c), 1)
    fwd_mask = ki <= qi

    def outputs(i, carry):
        rows = rows_of(i)
        qf = qf_ref[rows, :]
        qb = qb_ref[rows, :]
        scores = _dot_nt(jnp.concatenate([qf, qb], axis=0),
                         jnp.concatenate([kf_ref[rows, :], kb_ref[rows, :]], axis=0))
        p = jnp.where(fwd_mask, scores[:c, :c], scores[c:, c:]).astype(BF16)
        o_ref[rows, :] = (_dot(p, v_ref[rows, :])
                          + _dot_nt(jnp.concatenate([qf, qb], axis=1), snap_ref[i]))
        return carry

    lax.fori_loop(0, n, outputs, 0, unroll=True)


def _gla_scan(qf, kf, qb, kb, v, cv, *, batch, seq_len):
    tokens, dk_all = qf.shape
    dv_all = v.shape[1]
    dk = dk_all // GLA_HEADS
    dv = dv_all // GLA_HEADS
    n = seq_len // SCAN_CHUNK
    qk_spec = pl.BlockSpec((seq_len, dk), lambda b, h: (b, h))
    return pl.pallas_call(
        _gla_scan_kernel,
        grid=(batch, GLA_HEADS),
        in_specs=[qk_spec, qk_spec, qk_spec, qk_spec,
                  pl.BlockSpec((seq_len, dv), lambda b, h: (b, h)),
                  pl.BlockSpec((n, SUBLANES, dk), lambda b, h: (b, 0, h))],
        out_specs=pl.BlockSpec((seq_len, dv), lambda b, h: (b, h)),
        out_shape=jax.ShapeDtypeStruct((tokens, dv_all), F32),
        scratch_shapes=[
            pltpu.VMEM((n, dv, 2 * dk), F32),
            pltpu.VMEM((dv, 2 * dk), F32),
            pltpu.VMEM((n, dv, 2 * dk), BF16),
        ],
        compiler_params=pltpu.CompilerParams(
            dimension_semantics=("arbitrary", "arbitrary"), vmem_limit_bytes=VMEM_LIMIT_BYTES),
        name="gla_scan",
    )(qf, kf, qb, kb, v, cv)


def _gla_out_ffn_kernel(x_ref, o_ref, r_ref, hgain_ref, wout_ref, nffn_ref, wgu_ref, wd_ref,
                        nfin_ref, out_ref, act_ref, *, final):
    dv = hgain_ref.shape[1]
    gated = []
    for hh in range(o_ref.shape[1] // dv):
        cols = slice(hh * dv, (hh + 1) * dv)
        gated.append((_rms(o_ref[:, cols], hgain_ref[...]) * _silu(r_ref[:, cols])).astype(BF16))
    x1 = x_ref[...] + _dot(jnp.concatenate(gated, axis=1), wout_ref[...])
    y = _ffn(x1, nffn_ref[...], wgu_ref, wd_ref, act_ref)
    out_ref[...] = _rms(y, nfin_ref[...]) if final else y


def _gla_out_ffn(x2, o, r, hgain, wout, nffn, wgu, wd, nfin, *, final, tm):
    tokens, d = x2.shape
    d_ff = wd.shape[0]
    tile = lambda n: pl.BlockSpec((tm, n), lambda i: (i, 0))
    return pl.pallas_call(
        functools.partial(_gla_out_ffn_kernel, final=final),
        grid=(tokens // tm,),
        in_specs=[tile(d), tile(o.shape[1]), tile(r.shape[1])]
        + [_const_spec(a.shape) for a in (hgain, wout, nffn, wgu, wd, nfin)],
        out_specs=tile(d),
        out_shape=jax.ShapeDtypeStruct((tokens, d), F32),
        scratch_shapes=[pltpu.VMEM((tm, d_ff), BF16)],
        compiler_params=pltpu.CompilerParams(
            dimension_semantics=("arbitrary",), vmem_limit_bytes=VMEM_LIMIT_BYTES),
        name="gla_out_ffn",
    )(x2, o, r, hgain, wout, nffn, wgu, wd, nfin)


def _ffn_weights(w_gate, w_up, w_down):
    d, d_ff = w_gate.shape
    n = d_ff // FF_CHUNK
    wgu = jnp.concatenate([w_gate.reshape(d, n, FF_CHUNK), w_up.reshape(d, n, FF_CHUNK)], axis=-1)
    return wgu.transpose(1, 0, 2).astype(BF16), w_down.astype(BF16)


def kernel(x, norm_mix, norm_ffn, norm_final, w_pool, pool_scale, w_gla_in, w_gate_up, b_gate,
           gla_head_norm, w_gla_out, w_ffn_gate, w_ffn_up, w_ffn_down):
    batch, seq_len, d = x.shape
    depth = norm_mix.shape[0]
    key_dim = w_gate_up.shape[-1]
    val_dim = w_gla_out.shape[1]
    dk = key_dim // GLA_HEADS
    assert depth % 2 == 0, "the final norm is fused into the last GLA layer"
    assert seq_len % TOKEN_TILE == 0 and TOKEN_TILE % SCAN_CHUNK == 0
    assert d % (len(POOL_WINDOWS) * LANES) == 0 and dk % LANES == 0
    row = lambda a: a.reshape(1, -1)

    x2 = x.reshape(batch * seq_len, d)
    for i in range(depth):
        j = i // 2
        wgu, wd = _ffn_weights(w_ffn_gate[i], w_ffn_up[i], w_ffn_down[i])
        if i % 2 == 0:
            x2 = _pool_ffn(x2, row(norm_mix[i]), w_pool[j].astype(BF16), row(pool_scale[j]),
                           row(norm_ffn[i]), wgu, wd, seq_len=seq_len, tm=TOKEN_TILE)
        else:
            w_in = w_gla_in[j]
            i1 = key_dim
            i2 = i1 + key_dim
            i3 = i2 + val_dim
            i4 = i3 + val_dim
            wg = jnp.zeros((d, LANES), F32).at[:, :2 * GATE_RANK].set(w_in[:, i4:])
            wup = jnp.zeros((LANES, 2 * key_dim), F32)
            wup = wup.at[:GATE_RANK, :key_dim].set(w_gate_up[j, 0])
            wup = wup.at[GATE_RANK:2 * GATE_RANK, key_dim:].set(w_gate_up[j, 1])
            qf, kf, qb, kb, v, r, cv = _gla_in(
                x2, row(norm_mix[i]), w_in[:, :i1].astype(BF16), w_in[:, i1:i2].astype(BF16),
                w_in[:, i2:i3].astype(BF16), w_in[:, i3:i4].astype(BF16), wg.astype(BF16),
                wup.astype(BF16), b_gate[j].reshape(1, -1), q_scale=dk ** -0.5, tm=TOKEN_TILE)
            o = _gla_scan(qf, kf, qb, kb, v, cv, batch=batch, seq_len=seq_len)
            x2 = _gla_out_ffn(x2, o, r, row(gla_head_norm[j]), w_gla_out[j].astype(BF16),
                              row(norm_ffn[i]), wgu, wd, row(norm_final),
                              final=(i == depth - 1), tm=TOKEN_TILE)
    return x2.reshape(batch, seq_len, d)
```

```python
import functools

import jax
import jax.numpy as jnp
from jax import lax
from jax.experimental import pallas as pl
from jax.experimental.pallas import tpu as pltpu

F32 = jnp.float32
BF16 = jnp.bfloat16

EPS = 1e-6
POOL_WINDOWS = (2, 4, 8, 16)
GLA_HEADS = 4
GATE_RANK = 16
GATE_NORMALIZER = 16.0

LANES = 128
SUBLANES = 8
MXU_DIM = 256
VMEM_LIMIT_BYTES = 56 * 1024 * 1024

POOL_HALO = SUBLANES
FF_CHUNK = MXU_DIM
SCAN_CHUNK = 128
TOKEN_TILE = 512


def _rms(x, gain):
    ms = jnp.mean(x * x, axis=-1, keepdims=True)
    return x * lax.rsqrt(ms + EPS) * gain


def _silu(x):
    return x * jax.nn.sigmoid(x)


def _dot(a, b):
    return jnp.dot(a, b, preferred_element_type=F32)


def _dot_nt(a, b):
    return lax.dot_general(a, b, (((1,), (1,)), ((), ())), preferred_element_type=F32)


def _dot_tn(a, b):
    return lax.dot_general(a, b, (((0,), (0,)), ((), ())), preferred_element_type=F32)


def _ffn_stage(x1, o_ref, gain, wgu_ref, wd_ref, act_ref, final_gain=None, side_work=()):
    if side_work:
        o_ref[...] = x1
    h = _rms(x1, gain).astype(BF16)
    n_chunks = wgu_ref.shape[0]
    assert len(side_work) <= n_chunks
    work_after = {(j * n_chunks) // len(side_work): w for j, w in enumerate(side_work)}
    for c in range(n_chunks):
        gu = _dot(h, wgu_ref[c])
        g = gu[:, :FF_CHUNK]
        u = gu[:, FF_CHUNK:]
        act_ref[:, c * FF_CHUNK:(c + 1) * FF_CHUNK] = (_silu(g) * u).astype(BF16)
        if c in work_after:
            work_after[c]()
    y = (o_ref[...] if side_work else x1) + _dot(act_ref[...], wd_ref[...])
    o_ref[...] = y if final_gain is None else _rms(y, final_gain)


def _zero_at_first_step(*refs):
    @pl.when(pl.program_id(0) == 0)
    def _():
        for ref in refs:
            ref[...] = jnp.zeros_like(ref)


def _prepared_tile_spec(tm, n, n_tiles):
    return pl.BlockSpec((tm, n), lambda i: (jnp.minimum(i, n_tiles - 1), 0))


def _finished_tile_spec(tm, n):
    return pl.BlockSpec((tm, n), lambda i: (jnp.maximum(i - 1, 0), 0))


def _window_sum(s, win, tm):
    n = s.shape[0]
    assert win & (win - 1) == 0 and win <= 2 * POOL_HALO

    def ahead(a, k):
        return a if k == 0 else pltpu.roll(a, n - k, axis=0)

    width = 1
    while 2 * width < win:
        s = s + ahead(s, width)
        width *= 2
    start = POOL_HALO - win // 2
    return (ahead(s, start) + ahead(s, start + width))[:tm]


def _pool_ffn_kernel(x_ref, xprev_ref, xnext_ref, nmix_ref, wpool_ref, pscale_ref, nffn_ref,
                     wgu_ref, wd_ref, o_ref, hx_ref, x1_ref, act_ref, *, seq_len, n_tiles):
    _zero_at_first_step(x1_ref)
    tm = x_ref.shape[0]
    tiles_per_seq = seq_len // tm
    pos = jnp.minimum(pl.program_id(0), n_tiles - 1) % tiles_per_seq
    group = x_ref.shape[1] // len(POOL_WINDOWS)

    def normalise():
        gain = nmix_ref[...]
        hx_ref[POOL_HALO:POOL_HALO + tm, :] = _rms(x_ref[...], gain)
        hx_ref[0:POOL_HALO, :] = jnp.where(pos > 0, _rms(xprev_ref[...], gain), 0.0)
        hx_ref[POOL_HALO + tm:, :] = jnp.where(
            pos < tiles_per_seq - 1, _rms(xnext_ref[...], gain), 0.0)

    def mix_group(g, win):
        left = win // 2
        right = win - 1 - left
        cols = slice(g * group, (g + 1) * group)
        t = pos * tm + lax.broadcasted_iota(jnp.int32, (tm, 1), 0)
        acc = _window_sum(hx_ref[:, cols], win, tm)
        cnt = (jnp.minimum(t + right + 1, seq_len) - jnp.maximum(t - left, 0)).astype(F32)
        mixed = (acc / cnt - hx_ref[POOL_HALO:POOL_HALO + tm, cols]).astype(BF16)
        x1_ref[:, cols] = x_ref[:, cols] + _dot(mixed, wpool_ref[g]) * pscale_ref[:, cols]

    prepare = [normalise] + [functools.partial(mix_group, g, win)
                             for g, win in enumerate(POOL_WINDOWS)]
    _ffn_stage(x1_ref[...], o_ref, nffn_ref[...], wgu_ref, wd_ref, act_ref, side_work=prepare)


def _const_spec(shape):
    nd = len(shape)
    return pl.BlockSpec(shape, lambda *_: (0,) * nd, pipeline_mode=pl.Buffered(1))


def _pool_ffn(x2, nmix, wpool, pscale, nffn, wgu, wd, *, seq_len, tm):
    tokens, d = x2.shape
    n_tiles = tokens // tm
    halo_blocks = tokens // POOL_HALO
    per_tile = tm // POOL_HALO
    d_ff = wd.shape[0]
    tile_of = lambda i: jnp.minimum(i, n_tiles - 1)
    return pl.pallas_call(
        functools.partial(_pool_ffn_kernel, seq_len=seq_len, n_tiles=n_tiles),
        grid=(n_tiles + 1,),
        in_specs=[
            _prepared_tile_spec(tm, d, n_tiles),
            pl.BlockSpec((POOL_HALO, d), lambda i: (jnp.maximum(tile_of(i) * per_tile - 1, 0), 0)),
            pl.BlockSpec((POOL_HALO, d),
                         lambda i: (jnp.minimum((tile_of(i) + 1) * per_tile, halo_blocks - 1), 0)),
            _const_spec(nmix.shape),
            _const_spec(wpool.shape),
            _const_spec(pscale.shape),
            _const_spec(nffn.shape),
            _const_spec(wgu.shape),
            _const_spec(wd.shape),
        ],
        out_specs=_finished_tile_spec(tm, d),
        out_shape=jax.ShapeDtypeStruct((tokens, d), F32),
        scratch_shapes=[
            pltpu.VMEM((tm + 2 * POOL_HALO, d), F32),
            pltpu.VMEM((tm, d), F32),
            pltpu.VMEM((tm, d_ff), BF16),
        ],
        compiler_params=pltpu.CompilerParams(
            dimension_semantics=("arbitrary",), vmem_limit_bytes=VMEM_LIMIT_BYTES),
        name="pool_ffn",
    )(x2, x2, x2, nmix, wpool, pscale, nffn, wgu, wd)


def _centred_sum_matrices(c, scale):
    r = lax.broadcasted_iota(jnp.int32, (c, 2 * c), 0)
    j = lax.broadcasted_iota(jnp.int32, (c, 2 * c), 1) & (c - 1)
    one = lambda m: jnp.where(m, scale, 0.0)
    fwd = one(j <= r) - one(j < c // 2)
    bwd = one(j >= r) - one(j >= c // 2)
    return fwd.astype(BF16), bwd.astype(BF16)


def _split_bf16(x):
    hi = x.astype(BF16)
    lo = (x - hi.astype(F32)).astype(BF16)
    return jnp.concatenate([hi, lo], axis=0)


def _log_sigmoid(z):
    return jnp.minimum(z, 0.0) - jnp.log(1.0 + jnp.exp(-jnp.abs(z)))


def _gla_in_kernel(x_ref, nmix_ref, wq_ref, wk_ref, wv_ref, wr_ref, wg_ref, wup_ref, bg_ref,
                   qf_ref, kf_ref, qb_ref, kb_ref, v_ref, r_ref, cv_ref, q_ref, k_ref, ls_ref,
                   *, q_scale):
    _zero_at_first_step(q_ref, k_ref, ls_ref)
    tm = x_ref.shape[0]
    dk = wq_ref.shape[1]
    dv = wv_ref.shape[1]
    h = _rms(x_ref[...], nmix_ref[...]).astype(BF16)

    def project_v(cols):
        v_ref[:, cols] = _dot(h, wv_ref[:, cols]).astype(BF16)

    def project_r(cols):
        r_ref[:, cols] = _dot(h, wr_ref[:, cols])

    n_decay = tm // SCAN_CHUNK
    assert n_decay % 2 == 0
    piece = 2 * dv // n_decay
    output_projections = (
        [functools.partial(project_v, slice(j * piece, (j + 1) * piece)) for j in range(n_decay // 2)]
        + [functools.partial(project_r, slice(j * piece, (j + 1) * piece)) for j in range(n_decay // 2)])

    c = SCAN_CHUNK
    sum_fwd, sum_bwd = _centred_sum_matrices(c, 1.0 / GATE_NORMALIZER)
    vec_row = lax.broadcasted_iota(jnp.int32, (SUBLANES, dk), 0)
    for ci in range(n_decay):
        output_projections[ci]()
        rows = slice(ci * c, (ci + 1) * c)
        qc = q_ref[rows, :]
        kc = k_ref[rows, :]
        ls_f = ls_ref[rows, :dk]
        cf = _dot(sum_fwd, _split_bf16(ls_f))
        b_mid = ls_f[0:1] / GATE_NORMALIZER - cf[0:1]
        qf_ref[rows, :] = (qc * jnp.exp(cf)).astype(BF16)
        kf_ref[rows, :] = (kc * jnp.exp(-cf)).astype(BF16)
        ls_b = ls_ref[rows, dk:]
        cb = _dot(sum_bwd, _split_bf16(ls_b))
        s_mid = ls_b[c - 1:c] / GATE_NORMALIZER - cb[c - 1:c]
        qb_ref[rows, :] = (qc * jnp.exp(cb)).astype(BF16)
        kb_ref[rows, :] = (kc * jnp.exp(-cb)).astype(BF16)
        vecs = (jnp.exp(b_mid), jnp.exp(cf[c - 1:c]), jnp.exp(b_mid + cf[c - 1:c]),
                jnp.exp(s_mid), jnp.exp(cb[0:1]), jnp.exp(s_mid + cb[0:1]))
        cv = jnp.zeros((SUBLANES, dk), F32)
        for j, vec in enumerate(vecs):
            cv = jnp.where(vec_row == j, vec, cv)
        cv_ref[ci] = cv

    g_lr = _dot(h, wg_ref[...]).astype(BF16)
    ls_ref[...] = _log_sigmoid(_dot(g_lr, wup_ref[...]) + bg_ref[...])
    q_ref[...] = _dot(h, wq_ref[...]) * q_scale
    k_ref[...] = _dot(h, wk_ref[...])


def _gla_in(x2, nmix, wq, wk, wv, wr, wg, wup, bg, *, q_scale, tm):
    tokens, d = x2.shape
    dk = wq.shape[1]
    dv = wv.shape[1]
    n_chunks = tokens // SCAN_CHUNK
    n_tiles = tokens // tm
    decayed = _finished_tile_spec(tm, dk)
    projected = _prepared_tile_spec(tm, dv, n_tiles)
    return pl.pallas_call(
        functools.partial(_gla_in_kernel, q_scale=q_scale),
        grid=(n_tiles + 1,),
        in_specs=[_prepared_tile_spec(tm, d, n_tiles)]
        + [_const_spec(a.shape) for a in (nmix, wq, wk, wv, wr, wg, wup, bg)],
        out_specs=[decayed, decayed, decayed, decayed, projected, projected,
                   pl.BlockSpec((tm // SCAN_CHUNK, SUBLANES, dk),
                                lambda i: (jnp.maximum(i - 1, 0), 0, 0))],
        out_shape=[jax.ShapeDtypeStruct((tokens, dk), BF16)] * 4
        + [jax.ShapeDtypeStruct((tokens, dv), BF16), jax.ShapeDtypeStruct((tokens, dv), F32),
           jax.ShapeDtypeStruct((n_chunks, SUBLANES, dk), F32)],
        scratch_shapes=[pltpu.VMEM((tm, dk), F32), pltpu.VMEM((tm, dk), F32),
                        pltpu.VMEM((tm, 2 * dk), F32)],
        compiler_params=pltpu.CompilerParams(
            dimension_semantics=("arbitrary",), vmem_limit_bytes=VMEM_LIMIT_BYTES),
        name="gla_in",
    )(x2, nmix, wq, wk, wv, wr, wg, wup, bg)


def _gla_scan_kernel(qf_ref, kf_ref, qb_ref, kb_ref, v_ref, cv_ref, o_ref, ut_ref, st_ref, snap_ref):
    c = SCAN_CHUNK
    n = qf_ref.shape[0] // c
    dk = qf_ref.shape[1]

    def rows_of(i):
        return pl.ds(pl.multiple_of(i * c, c), c)

    def increments(i, carry):
        rows = rows_of(i)
        kk = jnp.concatenate([kf_ref[rows, :], kb_ref[rows, :]], axis=1)
        ut_ref[i] = _dot_tn(v_ref[rows, :], kk)
        return carry

    lax.fori_loop(0, n, increments, 0, unroll=True)

    st_ref[...] = jnp.zeros_like(st_ref)

    def recur(j, carry):
        jb = n - 1 - j
        cvf = cv_ref[j]
        cvb = cv_ref[jb]
        stf = st_ref[:, :dk]
        stb = st_ref[:, dk:]
        snap_ref[j, :, :dk] = (stf * cvf[0:1]).astype(BF16)
        snap_ref[jb, :, dk:] = (stb * cvb[3:4]).astype(BF16)
        st_ref[:, :dk] = stf * cvf[2:3] + ut_ref[j, :, :dk] * cvf[1:2]
        st_ref[:, dk:] = stb * cvb[5:6] + ut_ref[jb, :, dk:] * cvb[4:5]
        return carry

    lax.fori_loop(0, n, recur, 0, unroll=True)

    qi = lax.broadcasted_iota(jnp.int32, (c, c), 0)
    ki = lax.broadcasted_iota(jnp.int32, (c, c), 1)
    fwd_mask = ki <= qi

    def outputs(i, carry):
        rows = rows_of(i)
        qf = qf_ref[rows, :]
        qb = qb_ref[rows, :]
        scores = _dot_nt(jnp.concatenate([qf, qb], axis=0),
                         jnp.concatenate([kf_ref[rows, :], kb_ref[rows, :]], axis=0))
        p = jnp.where(fwd_mask, scores[:c, :c], scores[c:, c:]).astype(BF16)
        o_ref[rows, :] = (_dot(p, v_ref[rows, :])
                          + _dot_nt(jnp.concatenate([qf, qb], axis=1), snap_ref[i]))
        return carry

    lax.fori_loop(0, n, outputs, 0, unroll=True)


def _gla_scan(qf, kf, qb, kb, v, cv, *, batch, seq_len):
    tokens, dk_all = qf.shape
    dv_all = v.shape[1]
    dk = dk_all // GLA_HEADS
    dv = dv_all // GLA_HEADS
    n = seq_len // SCAN_CHUNK
    qk_spec = pl.BlockSpec((seq_len, dk), lambda b, h: (b, h))
    return pl.pallas_call(
        _gla_scan_kernel,
        grid=(batch, GLA_HEADS),
        in_specs=[qk_spec, qk_spec, qk_spec, qk_spec,
                  pl.BlockSpec((seq_len, dv), lambda b, h: (b, h)),
                  pl.BlockSpec((n, SUBLANES, dk), lambda b, h: (b, 0, h))],
        out_specs=pl.BlockSpec((seq_len, dv), lambda b, h: (b, h)),
        out_shape=jax.ShapeDtypeStruct((tokens, dv_all), F32),
        scratch_shapes=[
            pltpu.VMEM((n, dv, 2 * dk), F32),
            pltpu.VMEM((dv, 2 * dk), F32),
            pltpu.VMEM((n, dv, 2 * dk), BF16),
        ],
        compiler_params=pltpu.CompilerParams(
            dimension_semantics=("arbitrary", "arbitrary"), vmem_limit_bytes=VMEM_LIMIT_BYTES),
        name="gla_scan",
    )(qf, kf, qb, kb, v, cv)


def _gla_out_ffn_kernel(x_ref, o_ref, r_ref, hgain_ref, wout_ref, nffn_ref, wgu_ref, wd_ref,
                        nfin_ref, out_ref, act_ref, *, final):
    dv = hgain_ref.shape[1]
    gated = []
    for hh in range(o_ref.shape[1] // dv):
        cols = slice(hh * dv, (hh + 1) * dv)
        gated.append((_rms(o_ref[:, cols], hgain_ref[...]) * _silu(r_ref[:, cols])).astype(BF16))
    x1 = x_ref[...] + _dot(jnp.concatenate(gated, axis=1), wout_ref[...])
    _ffn_stage(x1, out_ref, nffn_ref[...], wgu_ref, wd_ref, act_ref,
               final_gain=nfin_ref[...] if final else None)


def _gla_out_ffn(x2, o, r, hgain, wout, nffn, wgu, wd, nfin, *, final, tm):
    tokens, d = x2.shape
    d_ff = wd.shape[0]
    tile = lambda n: pl.BlockSpec((tm, n), lambda i: (i, 0))
    return pl.pallas_call(
        functools.partial(_gla_out_ffn_kernel, final=final),
        grid=(tokens // tm,),
        in_specs=[tile(d), tile(o.shape[1]), tile(r.shape[1])]
        + [_const_spec(a.shape) for a in (hgain, wout, nffn, wgu, wd, nfin)],
        out_specs=tile(d),
        out_shape=jax.ShapeDtypeStruct((tokens, d), F32),
        scratch_shapes=[pltpu.VMEM((tm, d_ff), BF16)],
        compiler_params=pltpu.CompilerParams(
            dimension_semantics=("arbitrary",), vmem_limit_bytes=VMEM_LIMIT_BYTES),
        name="gla_out_ffn",
    )(x2, o, r, hgain, wout, nffn, wgu, wd, nfin)


def _ffn_weights(w_gate, w_up, w_down):
    d, d_ff = w_gate.shape
    n = d_ff // FF_CHUNK
    wgu = jnp.concatenate([w_gate.reshape(d, n, FF_CHUNK), w_up.reshape(d, n, FF_CHUNK)], axis=-1)
    return wgu.transpose(1, 0, 2).astype(BF16), w_down.astype(BF16)


def kernel(x, norm_mix, norm_ffn, norm_final, w_pool, pool_scale, w_gla_in, w_gate_up, b_gate,
           gla_head_norm, w_gla_out, w_ffn_gate, w_ffn_up, w_ffn_down):
    batch, seq_len, d = x.shape
    depth = norm_mix.shape[0]
    key_dim = w_gate_up.shape[-1]
    val_dim = w_gla_out.shape[1]
    dk = key_dim // GLA_HEADS
    assert depth % 2 == 0, "the final norm is fused into the last GLA layer"
    assert seq_len % TOKEN_TILE == 0 and TOKEN_TILE % SCAN_CHUNK == 0
    assert d % (len(POOL_WINDOWS) * LANES) == 0 and dk % LANES == 0
    row = lambda a: a.reshape(1, -1)

    x2 = x.reshape(batch * seq_len, d)
    for i in range(depth):
        j = i // 2
        wgu, wd = _ffn_weights(w_ffn_gate[i], w_ffn_up[i], w_ffn_down[i])
        if i % 2 == 0:
            x2 = _pool_ffn(x2, row(norm_mix[i]), w_pool[j].astype(BF16), row(pool_scale[j]),
                           row(norm_ffn[i]), wgu, wd, seq_len=seq_len, tm=TOKEN_TILE)
        else:
            w_in = w_gla_in[j]
            i1 = key_dim
            i2 = i1 + key_dim
            i3 = i2 + val_dim
            i4 = i3 + val_dim
            wg = jnp.zeros((d, LANES), F32).at[:, :2 * GATE_RANK].set(w_in[:, i4:])
            wup = jnp.zeros((LANES, 2 * key_dim), F32)
            wup = wup.at[:GATE_RANK, :key_dim].set(w_gate_up[j, 0])
            wup = wup.at[GATE_RANK:2 * GATE_RANK, key_dim:].set(w_gate_up[j, 1])
            qf, kf, qb, kb, v, r, cv = _gla_in(
                x2, row(norm_mix[i]), w_in[:, :i1].astype(BF16), w_in[:, i1:i2].astype(BF16),
                w_in[:, i2:i3].astype(BF16), w_in[:, i3:i4].astype(BF16), wg.astype(BF16),
                wup.astype(BF16), b_gate[j].reshape(1, -1), q_scale=dk ** -0.5, tm=TOKEN_TILE)
            o = _gla_scan(qf, kf, qb, kb, v, cv, batch=batch, seq_len=seq_len)
            x2 = _gla_out_ffn(x2, o, r, row(gla_head_norm[j]), w_gla_out[j].astype(BF16),
                              row(norm_ffn[i]), wgu, wd, row(norm_final),
                              final=(i == depth - 1), tm=TOKEN_TILE)
    return x2.reshape(batch, seq_len, d)
```

```python
import functools

import jax
import jax.numpy as jnp
from jax import lax
from jax.experimental import pallas as pl
from jax.experimental.pallas import tpu as pltpu

F32 = jnp.float32
BF16 = jnp.bfloat16

EPS = 1e-6
POOL_WINDOWS = (2, 4, 8, 16)
GLA_HEADS = 4
GATE_RANK = 16
GATE_NORMALIZER = 16.0

LANES = 128
SUBLANES = 8
MXU_DIM = 256
VMEM_LIMIT_BYTES = 56 * 1024 * 1024

POOL_HALO = SUBLANES
FF_CHUNK = MXU_DIM
SCAN_CHUNK = 128
TOKEN_TILE = 512


def _rms(x, gain):
    ms = jnp.mean(x * x, axis=-1, keepdims=True)
    return x * lax.rsqrt(ms + EPS) * gain


def _silu(x):
    return x * jax.nn.sigmoid(x)


def _dot(a, b):
    return jnp.dot(a, b, preferred_element_type=F32)


def _dot_nt(a, b):
    return lax.dot_general(a, b, (((1,), (1,)), ((), ())), preferred_element_type=F32)


def _dot_tn(a, b):
    return lax.dot_general(a, b, (((0,), (0,)), ((), ())), preferred_element_type=F32)


def _ffn_stage(x1, o_ref, gain, wg_ref, wu_ref, wd_ref, act_ref, final_gain=None, side_work=(),
               h_ref=None, after_gate_up=None):
    park_residual = bool(side_work) or after_gate_up is not None
    if park_residual:
        o_ref[...] = x1
    if h_ref is None:
        h = _rms(x1, gain).astype(BF16)
        read_h = lambda: h
    else:
        read_h = lambda: h_ref[...]
    d_ff = wg_ref.shape[1]
    chunks = [slice(s, min(s + FF_CHUNK, d_ff)) for s in range(0, d_ff, FF_CHUNK)]
    assert len(side_work) <= len(chunks)
    work_after = {(j * len(chunks)) // len(side_work): w for j, w in enumerate(side_work)}
    for c, cols in enumerate(chunks):
        g = _dot(read_h(), wg_ref[:, cols])
        u = _dot(read_h(), wu_ref[:, cols])
        act_ref[:, cols] = (_silu(g) * u).astype(BF16)
        if c in work_after:
            work_after[c]()
    if after_gate_up is not None:
        after_gate_up()
    y = (o_ref[...] if park_residual else x1) + _dot(act_ref[...], wd_ref[...])
    o_ref[...] = y if final_gain is None else _rms(y, final_gain)


def _zero_at_first_step(*refs):
    @pl.when(pl.program_id(0) == 0)
    def _():
        for ref in refs:
            ref[...] = jnp.zeros_like(ref)


def _prepared_tile_spec(tm, n, n_tiles):
    return pl.BlockSpec((tm, n), lambda i: (jnp.minimum(i, n_tiles - 1), 0))


def _finished_tile_spec(tm, n):
    return pl.BlockSpec((tm, n), lambda i: (jnp.maximum(i - 1, 0), 0))


def _window_sum(s, win, tm):
    n = s.shape[0]
    assert win & (win - 1) == 0 and win <= 2 * POOL_HALO

    def ahead(a, k):
        return a if k == 0 else pltpu.roll(a, n - k, axis=0)

    width = 1
    while 2 * width < win:
        s = s + ahead(s, width)
        width *= 2
    start = POOL_HALO - win // 2
    return (ahead(s, start) + ahead(s, start + width))[:tm]


def _pool_ffn_kernel(x_ref, xprev_ref, xnext_ref, nmix_ref, wpool_ref, pscale_ref, nffn_ref,
                     wg_ref, wu_ref, wd_ref, o_ref, hx_ref, x1_ref, h1_ref, act_ref,
                     *, seq_len, n_tiles):
    _zero_at_first_step(x1_ref, h1_ref)
    tm = x_ref.shape[0]
    tiles_per_seq = seq_len // tm
    pos = jnp.minimum(pl.program_id(0), n_tiles - 1) % tiles_per_seq
    group = x_ref.shape[1] // len(POOL_WINDOWS)

    def normalise():
        gain = nmix_ref[...]
        hx_ref[POOL_HALO:POOL_HALO + tm, :] = _rms(x_ref[...], gain)
        hx_ref[0:POOL_HALO, :] = jnp.where(pos > 0, _rms(xprev_ref[...], gain), 0.0)
        hx_ref[POOL_HALO + tm:, :] = jnp.where(
            pos < tiles_per_seq - 1, _rms(xnext_ref[...], gain), 0.0)

    def mix_group(g, win):
        left = win // 2
        right = win - 1 - left
        cols = slice(g * group, (g + 1) * group)
        t = pos * tm + lax.broadcasted_iota(jnp.int32, (tm, 1), 0)
        acc = _window_sum(hx_ref[:, cols], win, tm)
        cnt = (jnp.minimum(t + right + 1, seq_len) - jnp.maximum(t - left, 0)).astype(F32)
        mixed = (acc / cnt - hx_ref[POOL_HALO:POOL_HALO + tm, cols]).astype(BF16)
        x1_ref[:, cols] = x_ref[:, cols] + _dot(mixed, wpool_ref[g]) * pscale_ref[:, cols]

    prepare = [normalise] + [functools.partial(mix_group, g, win)
                             for g, win in enumerate(POOL_WINDOWS)]
    def normalise_prepared():
        h1_ref[...] = _rms(x1_ref[...], nffn_ref[...]).astype(BF16)

    _ffn_stage(x1_ref[...], o_ref, nffn_ref[...], wg_ref, wu_ref, wd_ref, act_ref,
               side_work=prepare, h_ref=h1_ref, after_gate_up=normalise_prepared)


def _const_spec(shape):
    nd = len(shape)
    return pl.BlockSpec(shape, lambda *_: (0,) * nd, pipeline_mode=pl.Buffered(1))


def _layer_spec(stacked, layer):
    _, rows, cols = stacked.shape
    return pl.BlockSpec((None, rows, cols), lambda *_: (layer, 0, 0), pipeline_mode=pl.Buffered(1))


def _pool_ffn(x2, nmix, wpool, pscale, nffn, ffn, layer, *, seq_len, tm):
    tokens, d = x2.shape
    n_tiles = tokens // tm
    halo_blocks = tokens // POOL_HALO
    per_tile = tm // POOL_HALO
    d_ff = ffn[2].shape[1]
    tile_of = lambda i: jnp.minimum(i, n_tiles - 1)
    return pl.pallas_call(
        functools.partial(_pool_ffn_kernel, seq_len=seq_len, n_tiles=n_tiles),
        grid=(n_tiles + 1,),
        in_specs=[
            _prepared_tile_spec(tm, d, n_tiles),
            pl.BlockSpec((POOL_HALO, d), lambda i: (jnp.maximum(tile_of(i) * per_tile - 1, 0), 0)),
            pl.BlockSpec((POOL_HALO, d),
                         lambda i: (jnp.minimum((tile_of(i) + 1) * per_tile, halo_blocks - 1), 0)),
            _const_spec(nmix.shape),
            _const_spec(wpool.shape),
            _const_spec(pscale.shape),
            _const_spec(nffn.shape),
        ] + [_layer_spec(w, layer) for w in ffn],
        out_specs=_finished_tile_spec(tm, d),
        out_shape=jax.ShapeDtypeStruct((tokens, d), F32),
        scratch_shapes=[
            pltpu.VMEM((tm + 2 * POOL_HALO, d), F32),
            pltpu.VMEM((tm, d), F32),
            pltpu.VMEM((tm, d), BF16),
            pltpu.VMEM((tm, d_ff), BF16),
        ],
        compiler_params=pltpu.CompilerParams(
            dimension_semantics=("arbitrary",), vmem_limit_bytes=VMEM_LIMIT_BYTES),
        name="pool_ffn",
    )(x2, x2, x2, nmix, wpool, pscale, nffn, *ffn)


def _centred_sum_matrices(c, scale):
    r = lax.broadcasted_iota(jnp.int32, (c, 2 * c), 0)
    j = lax.broadcasted_iota(jnp.int32, (c, 2 * c), 1) & (c - 1)
    one = lambda m: jnp.where(m, scale, 0.0)
    fwd = one(j <= r) - one(j < c // 2)
    bwd = one(j >= r) - one(j >= c // 2)
    return fwd.astype(BF16), bwd.astype(BF16)


def _split_bf16(x):
    hi = x.astype(BF16)
    lo = (x - hi.astype(F32)).astype(BF16)
    return jnp.concatenate([hi, lo], axis=0)


def _log_sigmoid(z):
    return jnp.minimum(z, 0.0) - jnp.log(1.0 + jnp.exp(-jnp.abs(z)))


def _gla_in_kernel(x_ref, nmix_ref, wq_ref, wk_ref, wv_ref, wrg_ref, wup_ref, bg_ref,
                   qf_ref, kf_ref, qb_ref, kb_ref, v_ref, r_ref, cv_ref, q_ref, k_ref, ls_ref,
                   *, q_scale):
    _zero_at_first_step(q_ref, k_ref, ls_ref)
    tm = x_ref.shape[0]
    dk = wq_ref.shape[1]
    dv = wv_ref.shape[1]
    c = SCAN_CHUNK
    n_decay = tm // c
    h = _rms(x_ref[...], nmix_ref[...]).astype(BF16)
    gate_lr = []

    def project_v(cols):
        v_ref[:, cols] = _dot(h, wv_ref[:, cols]).astype(BF16)

    def project_r(cols):
        r_ref[:, cols] = _dot(h, wrg_ref[:, cols])

    def project_r_and_gate(cols):
        rg = _dot(h, wrg_ref[:, cols.start:])
        r_ref[:, cols] = rg[:, :cols.stop - cols.start]
        gate_lr.append(rg[:, cols.stop - cols.start:].astype(BF16))

    assert n_decay % 2 == 0
    piece = 2 * dv // n_decay
    pieces = [slice(j * piece, (j + 1) * piece) for j in range(n_decay // 2)]
    output_projections = (
        [functools.partial(project_v, cols) for cols in pieces]
        + [functools.partial(project_r, cols) for cols in pieces[:-1]]
        + [functools.partial(project_r_and_gate, pieces[-1])])

    sum_fwd, sum_bwd = _centred_sum_matrices(c, 1.0 / GATE_NORMALIZER)
    vec_row = lax.broadcasted_iota(jnp.int32, (SUBLANES, dk), 0)
    for ci in range(n_decay):
        output_projections[ci]()
        rows = slice(ci * c, (ci + 1) * c)
        qc = q_ref[rows, :]
        kc = k_ref[rows, :]
        ls_f = ls_ref[rows, :dk]
        cf = _dot(sum_fwd, _split_bf16(ls_f))
        b_mid = ls_f[0:1] / GATE_NORMALIZER - cf[0:1]
        qf_ref[rows, :] = (qc * jnp.exp(cf)).astype(BF16)
        kf_ref[rows, :] = (kc * jnp.exp(-cf)).astype(BF16)
        ls_b = ls_ref[rows, dk:]
        cb = _dot(sum_bwd, _split_bf16(ls_b))
        s_mid = ls_b[c - 1:c] / GATE_NORMALIZER - cb[c - 1:c]
        qb_ref[rows, :] = (qc * jnp.exp(cb)).astype(BF16)
        kb_ref[rows, :] = (kc * jnp.exp(-cb)).astype(BF16)
        vecs = (jnp.exp(b_mid), jnp.exp(cf[c - 1:c]), jnp.exp(b_mid + cf[c - 1:c]),
                jnp.exp(s_mid), jnp.exp(cb[0:1]), jnp.exp(s_mid + cb[0:1]))
        cv = jnp.zeros((SUBLANES, dk), F32)
        for j, vec in enumerate(vecs):
            cv = jnp.where(vec_row == j, vec, cv)
        cv_ref[ci] = cv

    ls_ref[...] = _log_sigmoid(_dot(gate_lr[0], wup_ref[...]) + bg_ref[...])
    q_ref[...] = _dot(h, wq_ref[...]) * q_scale
    k_ref[...] = _dot(h, wk_ref[...])


def _gla_in(x2, nmix, wq, wk, wv, wrg, wup, bg, *, q_scale, tm):
    tokens, d = x2.shape
    dk = wq.shape[1]
    dv = wv.shape[1]
    n_chunks = tokens // SCAN_CHUNK
    n_tiles = tokens // tm
    decayed = _finished_tile_spec(tm, dk)
    projected = _prepared_tile_spec(tm, dv, n_tiles)
    return pl.pallas_call(
        functools.partial(_gla_in_kernel, q_scale=q_scale),
        grid=(n_tiles + 1,),
        in_specs=[_prepared_tile_spec(tm, d, n_tiles)]
        + [_const_spec(a.shape) for a in (nmix, wq, wk, wv, wrg, wup, bg)],
        out_specs=[decayed, decayed, decayed, decayed, projected, projected,
                   pl.BlockSpec((tm // SCAN_CHUNK, SUBLANES, dk),
                                lambda i: (jnp.maximum(i - 1, 0), 0, 0))],
        out_shape=[jax.ShapeDtypeStruct((tokens, dk), BF16)] * 4
        + [jax.ShapeDtypeStruct((tokens, dv), BF16), jax.ShapeDtypeStruct((tokens, dv), F32),
           jax.ShapeDtypeStruct((n_chunks, SUBLANES, dk), F32)],
        scratch_shapes=[pltpu.VMEM((tm, dk), F32), pltpu.VMEM((tm, dk), F32),
                        pltpu.VMEM((tm, 2 * dk), F32)],
        compiler_params=pltpu.CompilerParams(
            dimension_semantics=("arbitrary",), vmem_limit_bytes=VMEM_LIMIT_BYTES),
        name="gla_in",
    )(x2, nmix, wq, wk, wv, wrg, wup, bg)


def _gla_scan_kernel(qf_ref, kf_ref, qb_ref, kb_ref, v_ref, cv_ref, o_ref, ut_ref, st_ref, snap_ref):
    c = SCAN_CHUNK
    n = qf_ref.shape[0] // c
    dk = qf_ref.shape[1]

    def rows_of(i):
        return pl.ds(pl.multiple_of(i * c, c), c)

    def increments(i, carry):
        rows = rows_of(i)
        kk = jnp.concatenate([kf_ref[rows, :], kb_ref[rows, :]], axis=1)
        ut_ref[i] = _dot_tn(v_ref[rows, :], kk)
        return carry

    lax.fori_loop(0, n, increments, 0, unroll=True)

    st_ref[...] = jnp.zeros_like(st_ref)

    def recur(j, carry):
        jb = n - 1 - j
        cvf = cv_ref[j]
        cvb = cv_ref[jb]
        stf = st_ref[:, :dk]
        stb = st_ref[:, dk:]
        snap_ref[j, :, :dk] = (stf * cvf[0:1]).astype(BF16)
        snap_ref[jb, :, dk:] = (stb * cvb[3:4]).astype(BF16)
        st_ref[:, :dk] = stf * cvf[2:3] + ut_ref[j, :, :dk] * cvf[1:2]
        st_ref[:, dk:] = stb * cvb[5:6] + ut_ref[jb, :, dk:] * cvb[4:5]
        return carry

    lax.fori_loop(0, n, recur, 0, unroll=True)

    qi = lax.broadcasted_iota(jnp.int32, (c, c), 0)
    ki = lax.broadcasted_iota(jnp.int32, (c, c), 1)
    fwd_mask = ki <= qi

    def outputs(i, carry):
        rows = rows_of(i)
        qf = qf_ref[rows, :]
        qb = qb_ref[rows, :]
        scores = _dot_nt(jnp.concatenate([qf, qb], axis=0),
                         jnp.concatenate([kf_ref[rows, :], kb_ref[rows, :]], axis=0))
        p = jnp.where(fwd_mask, scores[:c, :c], scores[c:, c:]).astype(BF16)
        o_ref[rows, :] = (_dot(p, v_ref[rows, :])
                          + _dot_nt(jnp.concatenate([qf, qb], axis=1), snap_ref[i]))
        return carry

    lax.fori_loop(0, n, outputs, 0, unroll=True)


def _gla_scan(qf, kf, qb, kb, v, cv, *, batch, seq_len):
    tokens, dk_all = qf.shape
    dv_all = v.shape[1]
    dk = dk_all // GLA_HEADS
    dv = dv_all // GLA_HEADS
    n = seq_len // SCAN_CHUNK
    qk_spec = pl.BlockSpec((seq_len, dk), lambda b, h: (b, h))
    return pl.pallas_call(
        _gla_scan_kernel,
        grid=(batch, GLA_HEADS),
        in_specs=[qk_spec, qk_spec, qk_spec, qk_spec,
                  pl.BlockSpec((seq_len, dv), lambda b, h: (b, h)),
                  pl.BlockSpec((n, SUBLANES, dk), lambda b, h: (b, 0, h))],
        out_specs=pl.BlockSpec((seq_len, dv), lambda b, h: (b, h)),
        out_shape=jax.ShapeDtypeStruct((tokens, dv_all), F32),
        scratch_shapes=[
            pltpu.VMEM((n, dv, 2 * dk), F32),
            pltpu.VMEM((dv, 2 * dk), F32),
            pltpu.VMEM((n, dv, 2 * dk), BF16),
        ],
        compiler_params=pltpu.CompilerParams(
            dimension_semantics=("arbitrary", "arbitrary"), vmem_limit_bytes=VMEM_LIMIT_BYTES),
        name="gla_scan",
    )(qf, kf, qb, kb, v, cv)


def _gla_out_ffn_kernel(x_ref, o_ref, r_ref, hgain_ref, wout_ref, nffn_ref, nfin_ref,
                        wg_ref, wu_ref, wd_ref, out_ref, act_ref, *, final):
    dv = hgain_ref.shape[1]
    x1 = x_ref[...]
    for hh in range(o_ref.shape[1] // dv):
        cols = slice(hh * dv, (hh + 1) * dv)
        gated = (_rms(o_ref[:, cols], hgain_ref[...]) * _silu(r_ref[:, cols])).astype(BF16)
        x1 = x1 + _dot(gated, wout_ref[cols, :])
    _ffn_stage(x1, out_ref, nffn_ref[...], wg_ref, wu_ref, wd_ref, act_ref,
               final_gain=nfin_ref[...] if final else None)


def _gla_out_ffn(x2, o, r, hgain, wout, nffn, nfin, ffn, layer, *, final, tm):
    tokens, d = x2.shape
    d_ff = ffn[2].shape[1]
    tile = lambda n: pl.BlockSpec((tm, n), lambda i: (i, 0))
    return pl.pallas_call(
        functools.partial(_gla_out_ffn_kernel, final=final),
        grid=(tokens // tm,),
        in_specs=[tile(d), tile(o.shape[1]), tile(r.shape[1])]
        + [_const_spec(a.shape) for a in (hgain, wout, nffn, nfin)]
        + [_layer_spec(w, layer) for w in ffn],
        out_specs=tile(d),
        out_shape=jax.ShapeDtypeStruct((tokens, d), F32),
        scratch_shapes=[pltpu.VMEM((tm, d_ff), BF16)],
        compiler_params=pltpu.CompilerParams(
            dimension_semantics=("arbitrary",), vmem_limit_bytes=VMEM_LIMIT_BYTES),
        name="gla_out_ffn",
    )(x2, o, r, hgain, wout, nffn, nfin, *ffn)


def kernel(x, norm_mix, norm_ffn, norm_final, w_pool, pool_scale, w_gla_in, w_gate_up, b_gate,
           gla_head_norm, w_gla_out, w_ffn_gate, w_ffn_up, w_ffn_down):
    batch, seq_len, d = x.shape
    depth = norm_mix.shape[0]
    key_dim = w_gate_up.shape[-1]
    val_dim = w_gla_out.shape[1]
    dk = key_dim // GLA_HEADS
    assert depth % 2 == 0, "the final norm is fused into the last GLA layer"
    assert seq_len % TOKEN_TILE == 0 and TOKEN_TILE % SCAN_CHUNK == 0
    assert d % (len(POOL_WINDOWS) * LANES) == 0 and dk % LANES == 0
    row = lambda a: a.reshape(1, -1)

    x2 = x.reshape(batch * seq_len, d)
    ffn = (w_ffn_gate.astype(BF16), w_ffn_up.astype(BF16), w_ffn_down.astype(BF16))
    for i in range(depth):
        j = i // 2
        if i % 2 == 0:
            x2 = _pool_ffn(x2, row(norm_mix[i]), w_pool[j].astype(BF16), row(pool_scale[j]),
                           row(norm_ffn[i]), ffn, i, seq_len=seq_len, tm=TOKEN_TILE)
        else:
            w_in = w_gla_in[j]
            i1 = key_dim
            i2 = i1 + key_dim
            i3 = i2 + val_dim
            wrg = jnp.zeros((d, val_dim + LANES), F32).at[:, :val_dim + 2 * GATE_RANK].set(w_in[:, i3:])
            wup = jnp.zeros((LANES, 2 * key_dim), F32)
            wup = wup.at[:GATE_RANK, :key_dim].set(w_gate_up[j, 0])
            wup = wup.at[GATE_RANK:2 * GATE_RANK, key_dim:].set(w_gate_up[j, 1])
            qf, kf, qb, kb, v, r, cv = _gla_in(
                x2, row(norm_mix[i]), w_in[:, :i1].astype(BF16), w_in[:, i1:i2].astype(BF16),
                w_in[:, i2:i3].astype(BF16), wrg.astype(BF16),
                wup.astype(BF16), b_gate[j].reshape(1, -1), q_scale=dk ** -0.5, tm=TOKEN_TILE)
            o = _gla_scan(qf, kf, qb, kb, v, cv, batch=batch, seq_len=seq_len)
            x2 = _gla_out_ffn(x2, o, r, row(gla_head_norm[j]), w_gla_out[j].astype(BF16),
                              row(norm_ffn[i]), row(norm_final), ffn, i,
                              final=(i == depth - 1), tm=TOKEN_TILE)
    return x2.reshape(batch, seq_len, d)
```

```python
import functools

import jax
import jax.numpy as jnp
from jax import lax
from jax.experimental import pallas as pl
from jax.experimental.pallas import tpu as pltpu

F32 = jnp.float32
BF16 = jnp.bfloat16

EPS = 1e-6
POOL_WINDOWS = (2, 4, 8, 16)
GLA_HEADS = 4
GATE_RANK = 16
GATE_NORMALIZER = 16.0

LANES = 128
SUBLANES = 8
MXU_DIM = 256
VMEM_LIMIT_BYTES = 56 * 1024 * 1024

POOL_HALO = SUBLANES
FF_CHUNK = MXU_DIM
SCAN_CHUNK = 128
TOKEN_TILE = 512


def _rms(x, gain):
    ms = jnp.mean(x * x, axis=-1, keepdims=True)
    return x * lax.rsqrt(ms + EPS) * gain


def _silu(x):
    return x * jax.nn.sigmoid(x)


def _dot(a, b):
    return jnp.dot(a, b, preferred_element_type=F32)


def _dot_nt(a, b):
    return lax.dot_general(a, b, (((1,), (1,)), ((), ())), preferred_element_type=F32)


def _dot_tn(a, b):
    return lax.dot_general(a, b, (((0,), (0,)), ((), ())), preferred_element_type=F32)


def _ffn_stage(x1, o_ref, gain, wg_ref, wu_ref, wd_ref, act_ref, final_gain=None, side_work=(),
               h_ref=None, after_gate_up=None):
    park_residual = bool(side_work) or after_gate_up is not None
    if park_residual:
        o_ref[...] = x1
    if h_ref is None:
        h = _rms(x1, gain).astype(BF16)
        read_h = lambda: h
    else:
        read_h = lambda: h_ref[...]
    d_ff = wg_ref.shape[1]
    chunks = [slice(s, min(s + FF_CHUNK, d_ff)) for s in range(0, d_ff, FF_CHUNK)]
    assert len(side_work) <= len(chunks)
    work_after = {(j * len(chunks)) // len(side_work): w for j, w in enumerate(side_work)}
    for c, cols in enumerate(chunks):
        g = _dot(read_h(), wg_ref[:, cols])
        u = _dot(read_h(), wu_ref[:, cols])
        act_ref[:, cols] = (_silu(g) * u).astype(BF16)
        if c in work_after:
            work_after[c]()
    if after_gate_up is not None:
        after_gate_up()
    y = (o_ref[...] if park_residual else x1) + _dot(act_ref[...], wd_ref[...])
    o_ref[...] = y if final_gain is None else _rms(y, final_gain)


def _zero_at_first_step(*refs):
    @pl.when(pl.program_id(0) == 0)
    def _():
        for ref in refs:
            ref[...] = jnp.zeros_like(ref)


def _prepared_tile_spec(tm, n, n_tiles):
    return pl.BlockSpec((tm, n), lambda i: (jnp.minimum(i, n_tiles - 1), 0))


def _finished_tile_spec(tm, n):
    return pl.BlockSpec((tm, n), lambda i: (jnp.maximum(i - 1, 0), 0))


def _window_sum(s, win, tm):
    n = s.shape[0]
    assert win & (win - 1) == 0 and win <= 2 * POOL_HALO

    def ahead(a, k):
        return a if k == 0 else pltpu.roll(a, n - k, axis=0)

    width = 1
    while 2 * width < win:
        s = s + ahead(s, width)
        width *= 2
    start = POOL_HALO - win // 2
    return (ahead(s, start) + ahead(s, start + width))[:tm]


def _pool_ffn_kernel(x_ref, xprev_ref, xnext_ref, nmix_ref, wpool_ref, pscale_ref, nffn_ref,
                     wg_ref, wu_ref, wd_ref, o_ref, hx_ref, x1_ref, h1_ref, act_ref,
                     *, seq_len, n_tiles):
    _zero_at_first_step(x1_ref, h1_ref)
    tm = x_ref.shape[0]
    tiles_per_seq = seq_len // tm
    pos = jnp.minimum(pl.program_id(0), n_tiles - 1) % tiles_per_seq
    group = x_ref.shape[1] // len(POOL_WINDOWS)

    def normalise():
        gain = nmix_ref[...]
        hx_ref[POOL_HALO:POOL_HALO + tm, :] = _rms(x_ref[...], gain)
        hx_ref[0:POOL_HALO, :] = jnp.where(pos > 0, _rms(xprev_ref[...], gain), 0.0)
        hx_ref[POOL_HALO + tm:, :] = jnp.where(
            pos < tiles_per_seq - 1, _rms(xnext_ref[...], gain), 0.0)

    def mix_group(g, win):
        left = win // 2
        right = win - 1 - left
        cols = slice(g * group, (g + 1) * group)
        t = pos * tm + lax.broadcasted_iota(jnp.int32, (tm, 1), 0)
        acc = _window_sum(hx_ref[:, cols], win, tm)
        cnt = (jnp.minimum(t + right + 1, seq_len) - jnp.maximum(t - left, 0)).astype(F32)
        mixed = (acc / cnt - hx_ref[POOL_HALO:POOL_HALO + tm, cols]).astype(BF16)
        x1_ref[:, cols] = x_ref[:, cols] + _dot(mixed, wpool_ref[g]) * pscale_ref[:, cols]

    prepare = [normalise] + [functools.partial(mix_group, g, win)
                             for g, win in enumerate(POOL_WINDOWS)]
    def normalise_prepared():
        h1_ref[...] = _rms(x1_ref[...], nffn_ref[...]).astype(BF16)

    _ffn_stage(x1_ref[...], o_ref, nffn_ref[...], wg_ref, wu_ref, wd_ref, act_ref,
               side_work=prepare, h_ref=h1_ref, after_gate_up=normalise_prepared)


def _const_spec(shape):
    nd = len(shape)
    return pl.BlockSpec(shape, lambda *_: (0,) * nd, pipeline_mode=pl.Buffered(1))


def _layer_spec(stacked, layer):
    _, rows, cols = stacked.shape
    return pl.BlockSpec((None, rows, cols), lambda *_: (layer, 0, 0), pipeline_mode=pl.Buffered(1))


def _pool_ffn(x2, nmix, wpool, pscale, nffn, ffn, layer, *, seq_len, tm):
    tokens, d = x2.shape
    n_tiles = tokens // tm
    halo_blocks = tokens // POOL_HALO
    per_tile = tm // POOL_HALO
    d_ff = ffn[2].shape[1]
    tile_of = lambda i: jnp.minimum(i, n_tiles - 1)
    return pl.pallas_call(
        functools.partial(_pool_ffn_kernel, seq_len=seq_len, n_tiles=n_tiles),
        grid=(n_tiles + 1,),
        in_specs=[
            _prepared_tile_spec(tm, d, n_tiles),
            pl.BlockSpec((POOL_HALO, d), lambda i: (jnp.maximum(tile_of(i) * per_tile - 1, 0), 0)),
            pl.BlockSpec((POOL_HALO, d),
                         lambda i: (jnp.minimum((tile_of(i) + 1) * per_tile, halo_blocks - 1), 0)),
            _const_spec(nmix.shape),
            _const_spec(wpool.shape),
            _const_spec(pscale.shape),
            _const_spec(nffn.shape),
        ] + [_layer_spec(w, layer) for w in ffn],
        out_specs=_finished_tile_spec(tm, d),
        out_shape=jax.ShapeDtypeStruct((tokens, d), F32),
        scratch_shapes=[
            pltpu.VMEM((tm + 2 * POOL_HALO, d), F32),
            pltpu.VMEM((tm, d), F32),
            pltpu.VMEM((tm, d), BF16),
            pltpu.VMEM((tm, d_ff), BF16),
        ],
        compiler_params=pltpu.CompilerParams(
            dimension_semantics=("arbitrary",), vmem_limit_bytes=VMEM_LIMIT_BYTES),
        name="pool_ffn",
    )(x2, x2, x2, nmix, wpool, pscale, nffn, *ffn)


def _centred_sum_matrices(c, scale):
    r = lax.broadcasted_iota(jnp.int32, (c, 2 * c), 0)
    j = lax.broadcasted_iota(jnp.int32, (c, 2 * c), 1) & (c - 1)
    one = lambda m: jnp.where(m, scale, 0.0)
    fwd = one(j <= r) - one(j < c // 2)
    bwd = one(j >= r) - one(j >= c // 2)
    return fwd.astype(BF16), bwd.astype(BF16)


def _split_bf16(x):
    hi = x.astype(BF16)
    lo = (x - hi.astype(F32)).astype(BF16)
    return jnp.concatenate([hi, lo], axis=0)


def _log_sigmoid(z):
    return jnp.minimum(z, 0.0) - jnp.log(1.0 + jnp.exp(-jnp.abs(z)))


def _gla_in_kernel(x_ref, nmix_ref, wq_ref, wk_ref, wv_ref, wrg_ref, wup_ref, bg_ref,
                   qf_ref, kf_ref, qb_ref, kb_ref, v_ref, r_ref, cv_ref, q_ref, k_ref, ls_ref,
                   *, q_scale):
    _zero_at_first_step(q_ref, k_ref, ls_ref)
    tm = x_ref.shape[0]
    dk = wq_ref.shape[1]
    dv = wv_ref.shape[1]
    c = SCAN_CHUNK
    n_decay = tm // c
    h = _rms(x_ref[...], nmix_ref[...]).astype(BF16)
    gate_lr = []

    def project_v(cols):
        v_ref[:, cols] = _dot(h, wv_ref[:, cols]).astype(BF16)

    def project_r(cols):
        r_ref[:, cols] = _dot(h, wrg_ref[:, cols])

    def project_r_and_gate(cols):
        rg = _dot(h, wrg_ref[:, cols.start:])
        r_ref[:, cols] = rg[:, :cols.stop - cols.start]
        gate_lr.append(rg[:, cols.stop - cols.start:].astype(BF16))

    assert n_decay % 2 == 0
    piece = 2 * dv // n_decay
    pieces = [slice(j * piece, (j + 1) * piece) for j in range(n_decay // 2)]
    output_projections = (
        [functools.partial(project_v, cols) for cols in pieces]
        + [functools.partial(project_r, cols) for cols in pieces[:-1]]
        + [functools.partial(project_r_and_gate, pieces[-1])])

    sum_fwd, sum_bwd = _centred_sum_matrices(c, 1.0 / GATE_NORMALIZER)
    vec_row = lax.broadcasted_iota(jnp.int32, (SUBLANES, dk), 0)
    for ci in range(n_decay):
        output_projections[ci]()
        rows = slice(ci * c, (ci + 1) * c)
        qc = q_ref[rows, :]
        kc = k_ref[rows, :]
        ls_f = ls_ref[rows, :dk]
        cf = _dot(sum_fwd, _split_bf16(ls_f))
        b_mid = ls_f[0:1] / GATE_NORMALIZER - cf[0:1]
        qf_ref[rows, :] = (qc * jnp.exp(cf)).astype(BF16)
        kf_ref[rows, :] = (kc * jnp.exp(-cf)).astype(BF16)
        ls_b = ls_ref[rows, dk:]
        cb = _dot(sum_bwd, _split_bf16(ls_b))
        s_mid = ls_b[c - 1:c] / GATE_NORMALIZER - cb[c - 1:c]
        qb_ref[rows, :] = (qc * jnp.exp(cb)).astype(BF16)
        kb_ref[rows, :] = (kc * jnp.exp(-cb)).astype(BF16)
        vecs = (jnp.exp(b_mid), jnp.exp(cf[c - 1:c]), jnp.exp(b_mid + cf[c - 1:c]),
                jnp.exp(s_mid), jnp.exp(cb[0:1]), jnp.exp(s_mid + cb[0:1]))
        cv = jnp.zeros((SUBLANES, dk), F32)
        for j, vec in enumerate(vecs):
            cv = jnp.where(vec_row == j, vec, cv)
        cv_ref[ci] = cv

    ls_ref[...] = _log_sigmoid(_dot(gate_lr[0], wup_ref[...]) + bg_ref[...])
    q_ref[...] = _dot(h, wq_ref[...]) * q_scale
    k_ref[...] = _dot(h, wk_ref[...])


def _gla_in(x2, nmix, wq, wk, wv, wrg, wup, bg, *, q_scale, tm):
    tokens, d = x2.shape
    dk = wq.shape[1]
    dv = wv.shape[1]
    n_chunks = tokens // SCAN_CHUNK
    n_tiles = tokens // tm
    decayed = _finished_tile_spec(tm, dk)
    projected = _prepared_tile_spec(tm, dv, n_tiles)
    return pl.pallas_call(
        functools.partial(_gla_in_kernel, q_scale=q_scale),
        grid=(n_tiles + 1,),
        in_specs=[_prepared_tile_spec(tm, d, n_tiles)]
        + [_const_spec(a.shape) for a in (nmix, wq, wk, wv, wrg, wup, bg)],
        out_specs=[decayed, decayed, decayed, decayed, projected, projected,
                   pl.BlockSpec((tm // SCAN_CHUNK, SUBLANES, dk),
                                lambda i: (jnp.maximum(i - 1, 0), 0, 0))],
        out_shape=[jax.ShapeDtypeStruct((tokens, dk), BF16)] * 4
        + [jax.ShapeDtypeStruct((tokens, dv), BF16), jax.ShapeDtypeStruct((tokens, dv), F32),
           jax.ShapeDtypeStruct((n_chunks, SUBLANES, dk), F32)],
        scratch_shapes=[pltpu.VMEM((tm, dk), F32), pltpu.VMEM((tm, dk), F32),
                        pltpu.VMEM((tm, 2 * dk), F32)],
        compiler_params=pltpu.CompilerParams(
            dimension_semantics=("arbitrary",), vmem_limit_bytes=VMEM_LIMIT_BYTES),
        name="gla_in",
    )(x2, nmix, wq, wk, wv, wrg, wup, bg)


def _gla_scan_kernel(qf_ref, kf_ref, qb_ref, kb_ref, v_ref, cv_ref, o_ref, ut_ref, st_ref):
    c = SCAN_CHUNK
    n = qf_ref.shape[0] // c
    dk = qf_ref.shape[1]

    def both(cv, row):
        return jnp.concatenate([cv[row:row + 1], cv[row + 3:row + 4]], axis=1)

    def increment(i):
        rows = slice(i * c, (i + 1) * c)
        kk = jnp.concatenate([kf_ref[rows, :], kb_ref[rows, :]], axis=1)
        ut_ref[i] = _dot_tn(v_ref[rows, :], kk) * both(cv_ref[i], 1)

    def step_fwd(j):
        st_ref[j + 1, :, :dk] = st_ref[j, :, :dk] * cv_ref[j][2:3] + ut_ref[j, :, :dk]

    def step_bwd(j):
        st_ref[j - 1, :, dk:] = st_ref[j, :, dk:] * cv_ref[j][5:6] + ut_ref[j, :, dk:]

    assert n % 2 == 0
    st_ref[0, :, :dk] = jnp.zeros((st_ref.shape[1], dk), F32)
    st_ref[n - 1, :, dk:] = jnp.zeros((st_ref.shape[1], dk), F32)
    for t in range(n // 2):
        increment(t)
        increment(n - 1 - t)
        if t > 0:
            step_fwd(t - 1)
            step_bwd(n - t)
    for t in range(n // 2, n):
        step_fwd(t - 1)
        step_bwd(n - t)

    qi = lax.broadcasted_iota(jnp.int32, (c, c), 0)
    ki = lax.broadcasted_iota(jnp.int32, (c, c), 1)
    fwd_mask = ki <= qi
    no_keys = jnp.zeros((c, dk), BF16)

    def score(i):
        rows = slice(i * c, (i + 1) * c)
        qq = jnp.concatenate([qf_ref[rows, :], qb_ref[rows, :]], axis=1)
        keys = jnp.concatenate([jnp.concatenate([kf_ref[rows, :], no_keys], axis=1),
                                jnp.concatenate([no_keys, kb_ref[rows, :]], axis=1)], axis=0)
        return qq, _dot_nt(qq, keys)

    upcoming = score(0)
    for i in range(n):
        rows = slice(i * c, (i + 1) * c)
        qq, scores = upcoming
        if i + 1 < n:
            upcoming = score(i + 1)
        p = jnp.where(fwd_mask, scores[:, :c], scores[:, c:]).astype(BF16)
        seen = (st_ref[i] * both(cv_ref[i], 0)).astype(BF16)
        o_ref[rows, :] = _dot(p, v_ref[rows, :]) + _dot_nt(qq, seen)


def _gla_scan(qf, kf, qb, kb, v, cv, *, batch, seq_len):
    tokens, dk_all = qf.shape
    dv_all = v.shape[1]
    dk = dk_all // GLA_HEADS
    dv = dv_all // GLA_HEADS
    n = seq_len // SCAN_CHUNK
    qk_spec = pl.BlockSpec((seq_len, dk), lambda b, h: (b, h))
    return pl.pallas_call(
        _gla_scan_kernel,
        grid=(batch, GLA_HEADS),
        in_specs=[qk_spec, qk_spec, qk_spec, qk_spec,
                  pl.BlockSpec((seq_len, dv), lambda b, h: (b, h)),
                  pl.BlockSpec((n, SUBLANES, dk), lambda b, h: (b, 0, h))],
        out_specs=pl.BlockSpec((seq_len, dv), lambda b, h: (b, h)),
        out_shape=jax.ShapeDtypeStruct((tokens, dv_all), F32),
        scratch_shapes=[pltpu.VMEM((n, dv, 2 * dk), F32), pltpu.VMEM((n, dv, 2 * dk), F32)],
        compiler_params=pltpu.CompilerParams(
            dimension_semantics=("arbitrary", "arbitrary"), vmem_limit_bytes=VMEM_LIMIT_BYTES),
        name="gla_scan",
    )(qf, kf, qb, kb, v, cv)


def _gla_out_ffn_kernel(x_ref, o_ref, r_ref, hgain_ref, wout_ref, nffn_ref, nfin_ref,
                        wg_ref, wu_ref, wd_ref, out_ref, act_ref, *, final):
    dv = hgain_ref.shape[1]
    x1 = x_ref[...]
    for hh in range(o_ref.shape[1] // dv):
        cols = slice(hh * dv, (hh + 1) * dv)
        gated = (_rms(o_ref[:, cols], hgain_ref[...]) * _silu(r_ref[:, cols])).astype(BF16)
        x1 = x1 + _dot(gated, wout_ref[cols, :])
    _ffn_stage(x1, out_ref, nffn_ref[...], wg_ref, wu_ref, wd_ref, act_ref,
               final_gain=nfin_ref[...] if final else None)


def _gla_out_ffn(x2, o, r, hgain, wout, nffn, nfin, ffn, layer, *, final, tm):
    tokens, d = x2.shape
    d_ff = ffn[2].shape[1]
    tile = lambda n: pl.BlockSpec((tm, n), lambda i: (i, 0))
    return pl.pallas_call(
        functools.partial(_gla_out_ffn_kernel, final=final),
        grid=(tokens // tm,),
        in_specs=[tile(d), tile(o.shape[1]), tile(r.shape[1])]
        + [_const_spec(a.shape) for a in (hgain, wout, nffn, nfin)]
        + [_layer_spec(w, layer) for w in ffn],
        out_specs=tile(d),
        out_shape=jax.ShapeDtypeStruct((tokens, d), F32),
        scratch_shapes=[pltpu.VMEM((tm, d_ff), BF16)],
        compiler_params=pltpu.CompilerParams(
            dimension_semantics=("arbitrary",), vmem_limit_bytes=VMEM_LIMIT_BYTES),
        name="gla_out_ffn",
    )(x2, o, r, hgain, wout, nffn, nfin, *ffn)


def kernel(x, norm_mix, norm_ffn, norm_final, w_pool, pool_scale, w_gla_in, w_gate_up, b_gate,
           gla_head_norm, w_gla_out, w_ffn_gate, w_ffn_up, w_ffn_down):
    batch, seq_len, d = x.shape
    depth = norm_mix.shape[0]
    key_dim = w_gate_up.shape[-1]
    val_dim = w_gla_out.shape[1]
    dk = key_dim // GLA_HEADS
    assert depth % 2 == 0, "the final norm is fused into the last GLA layer"
    assert seq_len % TOKEN_TILE == 0 and TOKEN_TILE % SCAN_CHUNK == 0
    assert d % (len(POOL_WINDOWS) * LANES) == 0 and dk % LANES == 0
    row = lambda a: a.reshape(1, -1)

    x2 = x.reshape(batch * seq_len, d)
    ffn = (w_ffn_gate.astype(BF16), w_ffn_up.astype(BF16), w_ffn_down.astype(BF16))
    for i in range(depth):
        j = i // 2
        if i % 2 == 0:
            x2 = _pool_ffn(x2, row(norm_mix[i]), w_pool[j].astype(BF16), row(pool_scale[j]),
                           row(norm_ffn[i]), ffn, i, seq_len=seq_len, tm=TOKEN_TILE)
        else:
            w_in = w_gla_in[j]
            i1 = key_dim
            i2 = i1 + key_dim
            i3 = i2 + val_dim
            wrg = jnp.zeros((d, val_dim + LANES), F32).at[:, :val_dim + 2 * GATE_RANK].set(w_in[:, i3:])
            wup = jnp.zeros((LANES, 2 * key_dim), F32)
            wup = wup.at[:GATE_RANK, :key_dim].set(w_gate_up[j, 0])
            wup = wup.at[GATE_RANK:2 * GATE_RANK, key_dim:].set(w_gate_up[j, 1])
            qf, kf, qb, kb, v, r, cv = _gla_in(
                x2, row(norm_mix[i]), w_in[:, :i1].astype(BF16), w_in[:, i1:i2].astype(BF16),
                w_in[:, i2:i3].astype(BF16), wrg.astype(BF16),
                wup.astype(BF16), b_gate[j].reshape(1, -1), q_scale=dk ** -0.5, tm=TOKEN_TILE)
            o = _gla_scan(qf, kf, qb, kb, v, cv, batch=batch, seq_len=seq_len)
            x2 = _gla_out_ffn(x2, o, r, row(gla_head_norm[j]), w_gla_out[j].astype(BF16),
                              row(norm_ffn[i]), row(norm_final), ffn, i,
                              final=(i == depth - 1), tm=TOKEN_TILE)
    return x2.reshape(batch, seq_len, d)
```

```python
import functools

import jax
import jax.numpy as jnp
from jax import lax
from jax.experimental import pallas as pl
from jax.experimental.pallas import tpu as pltpu

F32 = jnp.float32
BF16 = jnp.bfloat16

EPS = 1e-6
POOL_WINDOWS = (2, 4, 8, 16)
GLA_HEADS = 4
GATE_RANK = 16
GATE_NORMALIZER = 16.0

LANES = 128
SUBLANES = 8
MXU_DIM = 256
VMEM_LIMIT_BYTES = 56 * 1024 * 1024

POOL_HALO = SUBLANES
FF_CHUNK = MXU_DIM
SCAN_CHUNK = 128
TOKEN_TILE = 512


def _rms(x, gain):
    ms = jnp.mean(x * x, axis=-1, keepdims=True)
    return x * lax.rsqrt(ms + EPS) * gain


def _silu(x):
    return x * jax.nn.sigmoid(x)


def _dot(a, b):
    return jnp.dot(a, b, preferred_element_type=F32)


def _dot_nt(a, b):
    return lax.dot_general(a, b, (((1,), (1,)), ((), ())), preferred_element_type=F32)


def _dot_tn(a, b):
    return lax.dot_general(a, b, (((0,), (0,)), ((), ())), preferred_element_type=F32)


def _ffn_stage(x1, o_ref, gain, wg_ref, wu_ref, wd_ref, act_ref, final_gain=None, side_work=(),
               h_ref=None, after_gate_up=None):
    park_residual = bool(side_work) or after_gate_up is not None
    if park_residual:
        o_ref[...] = x1
    if h_ref is None:
        h = _rms(x1, gain).astype(BF16)
        read_h = lambda: h
    else:
        read_h = lambda: h_ref[...]
    d_ff = wg_ref.shape[1]
    chunks = [slice(s, min(s + FF_CHUNK, d_ff)) for s in range(0, d_ff, FF_CHUNK)]
    assert len(side_work) <= len(chunks)
    work_after = {(j * len(chunks)) // len(side_work): w for j, w in enumerate(side_work)}
    for c, cols in enumerate(chunks):
        g = _dot(read_h(), wg_ref[:, cols])
        u = _dot(read_h(), wu_ref[:, cols])
        act_ref[:, cols] = (_silu(g) * u).astype(BF16)
        if c in work_after:
            work_after[c]()
    if after_gate_up is not None:
        after_gate_up()
    y = (o_ref[...] if park_residual else x1) + _dot(act_ref[...], wd_ref[...])
    o_ref[...] = y if final_gain is None else _rms(y, final_gain)


def _zero_at_first_step(*refs):
    @pl.when(pl.program_id(0) == 0)
    def _():
        for ref in refs:
            ref[...] = jnp.zeros_like(ref)


def _prepared_tile_spec(tm, n, n_tiles):
    return pl.BlockSpec((tm, n), lambda i: (jnp.minimum(i, n_tiles - 1), 0))


def _finished_tile_spec(tm, n):
    return pl.BlockSpec((tm, n), lambda i: (jnp.maximum(i - 1, 0), 0))


def _window_sum(s, win, tm):
    n = s.shape[0]
    assert win & (win - 1) == 0 and win <= 2 * POOL_HALO

    def ahead(a, k):
        return a if k == 0 else pltpu.roll(a, n - k, axis=0)

    width = 1
    while 2 * width < win:
        s = s + ahead(s, width)
        width *= 2
    start = POOL_HALO - win // 2
    return (ahead(s, start) + ahead(s, start + width))[:tm]


def _pool_ffn_kernel(x_ref, xprev_ref, xnext_ref, nmix_ref, wpool_ref, pscale_ref, nffn_ref,
                     wg_ref, wu_ref, wd_ref, o_ref, hx_ref, x1_ref, h1_ref, act_ref,
                     *, seq_len, n_tiles):
    _zero_at_first_step(x1_ref, h1_ref)
    tm = x_ref.shape[0]
    tiles_per_seq = seq_len // tm
    pos = jnp.minimum(pl.program_id(0), n_tiles - 1) % tiles_per_seq
    group = x_ref.shape[1] // len(POOL_WINDOWS)

    def normalise():
        gain = nmix_ref[...]
        hx_ref[POOL_HALO:POOL_HALO + tm, :] = _rms(x_ref[...], gain)
        hx_ref[0:POOL_HALO, :] = jnp.where(pos > 0, _rms(xprev_ref[...], gain), 0.0)
        hx_ref[POOL_HALO + tm:, :] = jnp.where(
            pos < tiles_per_seq - 1, _rms(xnext_ref[...], gain), 0.0)

    def mix_group(g, win):
        left = win // 2
        right = win - 1 - left
        cols = slice(g * group, (g + 1) * group)
        t = pos * tm + lax.broadcasted_iota(jnp.int32, (tm, 1), 0)
        acc = _window_sum(hx_ref[:, cols], win, tm)
        cnt = (jnp.minimum(t + right + 1, seq_len) - jnp.maximum(t - left, 0)).astype(F32)
        mixed = (acc / cnt - hx_ref[POOL_HALO:POOL_HALO + tm, cols]).astype(BF16)
        x1_ref[:, cols] = x_ref[:, cols] + _dot(mixed, wpool_ref[g]) * pscale_ref[:, cols]

    prepare = [normalise] + [functools.partial(mix_group, g, win)
                             for g, win in enumerate(POOL_WINDOWS)]
    def normalise_prepared():
        h1_ref[...] = _rms(x1_ref[...], nffn_ref[...]).astype(BF16)

    _ffn_stage(x1_ref[...], o_ref, nffn_ref[...], wg_ref, wu_ref, wd_ref, act_ref,
               side_work=prepare, h_ref=h1_ref, after_gate_up=normalise_prepared)


def _const_spec(shape):
    nd = len(shape)
    return pl.BlockSpec(shape, lambda *_: (0,) * nd, pipeline_mode=pl.Buffered(1))


def _layer_spec(stacked, layer):
    _, rows, cols = stacked.shape
    return pl.BlockSpec((None, rows, cols), lambda *_: (layer, 0, 0), pipeline_mode=pl.Buffered(1))


def _pool_ffn(x2, nmix, wpool, pscale, nffn, ffn, layer, *, seq_len, tm):
    tokens, d = x2.shape
    n_tiles = tokens // tm
    halo_blocks = tokens // POOL_HALO
    per_tile = tm // POOL_HALO
    d_ff = ffn[2].shape[1]
    tile_of = lambda i: jnp.minimum(i, n_tiles - 1)
    return pl.pallas_call(
        functools.partial(_pool_ffn_kernel, seq_len=seq_len, n_tiles=n_tiles),
        grid=(n_tiles + 1,),
        in_specs=[
            _prepared_tile_spec(tm, d, n_tiles),
            pl.BlockSpec((POOL_HALO, d), lambda i: (jnp.maximum(tile_of(i) * per_tile - 1, 0), 0)),
            pl.BlockSpec((POOL_HALO, d),
                         lambda i: (jnp.minimum((tile_of(i) + 1) * per_tile, halo_blocks - 1), 0)),
            _const_spec(nmix.shape),
            _const_spec(wpool.shape),
            _const_spec(pscale.shape),
            _const_spec(nffn.shape),
        ] + [_layer_spec(w, layer) for w in ffn],
        out_specs=_finished_tile_spec(tm, d),
        out_shape=jax.ShapeDtypeStruct((tokens, d), F32),
        scratch_shapes=[
            pltpu.VMEM((tm + 2 * POOL_HALO, d), F32),
            pltpu.VMEM((tm, d), F32),
            pltpu.VMEM((tm, d), BF16),
            pltpu.VMEM((tm, d_ff), BF16),
        ],
        compiler_params=pltpu.CompilerParams(
            dimension_semantics=("arbitrary",), vmem_limit_bytes=VMEM_LIMIT_BYTES),
        name="pool_ffn",
    )(x2, x2, x2, nmix, wpool, pscale, nffn, *ffn)


def _centred_sum_matrices(c, scale):
    r = lax.broadcasted_iota(jnp.int32, (c, 2 * c), 0)
    j = lax.broadcasted_iota(jnp.int32, (c, 2 * c), 1) & (c - 1)
    one = lambda m: jnp.where(m, scale, 0.0)
    fwd = one(j <= r) - one(j < c // 2)
    bwd = one(j >= r) - one(j >= c // 2)
    return fwd.astype(BF16), bwd.astype(BF16)


def _split_bf16(x):
    hi = x.astype(BF16)
    lo = (x - hi.astype(F32)).astype(BF16)
    return jnp.concatenate([hi, lo], axis=0)


def _log_sigmoid(z):
    return jnp.minimum(z, 0.0) - jnp.log(1.0 + jnp.exp(-jnp.abs(z)))


def _gla_in_kernel(x_ref, nmix_ref, wq_ref, wk_ref, wv_ref, wrg_ref, wup_ref, bg_ref,
                   qf_ref, kf_ref, qb_ref, kb_ref, v_ref, r_ref, cv_ref, q_ref, k_ref, ls_ref,
                   *, q_scale):
    _zero_at_first_step(q_ref, k_ref, ls_ref)
    tm = x_ref.shape[0]
    dk = wq_ref.shape[1]
    dv = wv_ref.shape[1]
    c = SCAN_CHUNK
    n_decay = tm // c
    h = _rms(x_ref[...], nmix_ref[...]).astype(BF16)
    gate_lr = []

    def by_head(ref, rows, cols, val):
        width = ref.shape[2]
        for hh in range(cols.start // width, cols.stop // width):
            ref[hh, rows, :] = val[:, hh * width - cols.start:(hh + 1) * width - cols.start]

    def project_v(cols):
        by_head(v_ref, slice(0, tm), cols, _dot(h, wv_ref[:, cols]).astype(BF16))

    def project_r(cols):
        r_ref[:, cols] = _dot(h, wrg_ref[:, cols])

    def project_r_and_gate(cols):
        rg = _dot(h, wrg_ref[:, cols.start:])
        r_ref[:, cols] = rg[:, :cols.stop - cols.start]
        gate_lr.append(rg[:, cols.stop - cols.start:].astype(BF16))

    assert n_decay % 2 == 0
    piece = 2 * dv // n_decay
    pieces = [slice(j * piece, (j + 1) * piece) for j in range(n_decay // 2)]
    output_projections = (
        [functools.partial(project_v, cols) for cols in pieces]
        + [functools.partial(project_r, cols) for cols in pieces[:-1]]
        + [functools.partial(project_r_and_gate, pieces[-1])])

    sum_fwd, sum_bwd = _centred_sum_matrices(c, 1.0 / GATE_NORMALIZER)
    vec_row = lax.broadcasted_iota(jnp.int32, (SUBLANES, dk), 0)
    for ci in range(n_decay):
        output_projections[ci]()
        rows = slice(ci * c, (ci + 1) * c)
        qc = q_ref[rows, :]
        kc = k_ref[rows, :]
        ls_f = ls_ref[rows, :dk]
        cf = _dot(sum_fwd, _split_bf16(ls_f))
        b_mid = ls_f[0:1] / GATE_NORMALIZER - cf[0:1]
        by_head(qf_ref, rows, slice(0, dk), (qc * jnp.exp(cf)).astype(BF16))
        by_head(kf_ref, rows, slice(0, dk), (kc * jnp.exp(-cf)).astype(BF16))
        ls_b = ls_ref[rows, dk:]
        cb = _dot(sum_bwd, _split_bf16(ls_b))
        s_mid = ls_b[c - 1:c] / GATE_NORMALIZER - cb[c - 1:c]
        by_head(qb_ref, rows, slice(0, dk), (qc * jnp.exp(cb)).astype(BF16))
        by_head(kb_ref, rows, slice(0, dk), (kc * jnp.exp(-cb)).astype(BF16))
        vecs = (jnp.exp(b_mid), jnp.exp(cf[c - 1:c]), jnp.exp(b_mid + cf[c - 1:c]),
                jnp.exp(s_mid), jnp.exp(cb[0:1]), jnp.exp(s_mid + cb[0:1]))
        cv = jnp.zeros((SUBLANES, dk), F32)
        for j, vec in enumerate(vecs):
            cv = jnp.where(vec_row == j, vec, cv)
        cv_ref[ci] = cv

    ls_ref[...] = _log_sigmoid(_dot(gate_lr[0], wup_ref[...]) + bg_ref[...])
    q_ref[...] = _dot(h, wq_ref[...]) * q_scale
    k_ref[...] = _dot(h, wk_ref[...])


def _gla_in(x2, nmix, wq, wk, wv, wrg, wup, bg, *, q_scale, tm):
    tokens, d = x2.shape
    dk = wq.shape[1]
    dv = wv.shape[1]
    n_chunks = tokens // SCAN_CHUNK
    n_tiles = tokens // tm
    heads = GLA_HEADS
    decayed = pl.BlockSpec((heads, tm, dk // heads), lambda i: (0, jnp.maximum(i - 1, 0), 0))
    v_spec = pl.BlockSpec((heads, tm, dv // heads), lambda i: (0, jnp.minimum(i, n_tiles - 1), 0))
    projected = _prepared_tile_spec(tm, dv, n_tiles)
    return pl.pallas_call(
        functools.partial(_gla_in_kernel, q_scale=q_scale),
        grid=(n_tiles + 1,),
        in_specs=[_prepared_tile_spec(tm, d, n_tiles)]
        + [_const_spec(a.shape) for a in (nmix, wq, wk, wv, wrg, wup, bg)],
        out_specs=[decayed, decayed, decayed, decayed, v_spec, projected,
                   pl.BlockSpec((tm // SCAN_CHUNK, SUBLANES, dk),
                                lambda i: (jnp.maximum(i - 1, 0), 0, 0))],
        out_shape=[jax.ShapeDtypeStruct((heads, tokens, dk // heads), BF16)] * 4
        + [jax.ShapeDtypeStruct((heads, tokens, dv // heads), BF16),
           jax.ShapeDtypeStruct((tokens, dv), F32),
           jax.ShapeDtypeStruct((n_chunks, SUBLANES, dk), F32)],
        scratch_shapes=[pltpu.VMEM((tm, dk), F32), pltpu.VMEM((tm, dk), F32),
                        pltpu.VMEM((tm, 2 * dk), F32)],
        compiler_params=pltpu.CompilerParams(
            dimension_semantics=("arbitrary",), vmem_limit_bytes=VMEM_LIMIT_BYTES),
        name="gla_in",
    )(x2, nmix, wq, wk, wv, wrg, wup, bg)


def _gla_scan_kernel(qf_ref, kf_ref, qb_ref, kb_ref, v_ref, cv_ref, o_ref, ut_ref, st_ref):
    c = SCAN_CHUNK
    n = qf_ref.shape[0] // c
    dk = qf_ref.shape[1]

    def both(cv, row):
        return jnp.concatenate([cv[row:row + 1], cv[row + 3:row + 4]], axis=1)

    def increment(i):
        rows = slice(i * c, (i + 1) * c)
        kk = jnp.concatenate([kf_ref[rows, :], kb_ref[rows, :]], axis=1)
        ut_ref[i] = _dot_tn(v_ref[rows, :], kk) * both(cv_ref[i], 1)

    def step_fwd(j):
        st_ref[j + 1, :, :dk] = st_ref[j, :, :dk] * cv_ref[j][2:3] + ut_ref[j, :, :dk]

    def step_bwd(j):
        st_ref[j - 1, :, dk:] = st_ref[j, :, dk:] * cv_ref[j][5:6] + ut_ref[j, :, dk:]

    assert n % 2 == 0
    st_ref[0, :, :dk] = jnp.zeros((st_ref.shape[1], dk), F32)
    st_ref[n - 1, :, dk:] = jnp.zeros((st_ref.shape[1], dk), F32)
    for t in range(n // 2):
        increment(t)
        increment(n - 1 - t)
        if t > 0:
            step_fwd(t - 1)
            step_bwd(n - t)
    for t in range(n // 2, n):
        step_fwd(t - 1)
        step_bwd(n - t)

    qi = lax.broadcasted_iota(jnp.int32, (c, c), 0)
    ki = lax.broadcasted_iota(jnp.int32, (c, c), 1)
    fwd_mask = ki <= qi
    no_keys = jnp.zeros((c, dk), BF16)

    def score(i):
        rows = slice(i * c, (i + 1) * c)
        qq = jnp.concatenate([qf_ref[rows, :], qb_ref[rows, :]], axis=1)
        keys = jnp.concatenate([jnp.concatenate([kf_ref[rows, :], no_keys], axis=1),
                                jnp.concatenate([no_keys, kb_ref[rows, :]], axis=1)], axis=0)
        return qq, _dot_nt(qq, keys)

    upcoming = score(0)
    for i in range(n):
        rows = slice(i * c, (i + 1) * c)
        qq, scores = upcoming
        if i + 1 < n:
            upcoming = score(i + 1)
        p = jnp.where(fwd_mask, scores[:, :c], scores[:, c:]).astype(BF16)
        seen = (st_ref[i] * both(cv_ref[i], 0)).astype(BF16)
        o_ref[rows, :] = _dot(p, v_ref[rows, :]) + _dot_nt(qq, seen)


def _gla_scan(qf, kf, qb, kb, v, cv, *, batch, seq_len):
    heads, tokens, dk = qf.shape
    dv = v.shape[2]
    n = seq_len // SCAN_CHUNK
    head_major = lambda width: pl.BlockSpec((None, seq_len, width), lambda b, h: (h, b, 0))
    qk_spec = head_major(dk)
    return pl.pallas_call(
        _gla_scan_kernel,
        grid=(batch, heads),
        in_specs=[qk_spec, qk_spec, qk_spec, qk_spec, head_major(dv),
                  pl.BlockSpec((n, SUBLANES, dk), lambda b, h: (b, 0, h))],
        out_specs=head_major(dv),
        out_shape=jax.ShapeDtypeStruct((heads, tokens, dv), F32),
        scratch_shapes=[pltpu.VMEM((n, dv, 2 * dk), F32), pltpu.VMEM((n, dv, 2 * dk), F32)],
        compiler_params=pltpu.CompilerParams(
            dimension_semantics=("arbitrary", "arbitrary"), vmem_limit_bytes=VMEM_LIMIT_BYTES),
        name="gla_scan",
    )(qf, kf, qb, kb, v, cv)


def _gla_out_ffn_kernel(x_ref, o_ref, r_ref, hgain_ref, wout_ref, nffn_ref, nfin_ref,
                        wg_ref, wu_ref, wd_ref, out_ref, act_ref, *, final):
    dv = hgain_ref.shape[1]
    x1 = x_ref[...]
    for hh in range(o_ref.shape[0]):
        cols = slice(hh * dv, (hh + 1) * dv)
        gated = (_rms(o_ref[hh], hgain_ref[...]) * _silu(r_ref[:, cols])).astype(BF16)
        x1 = x1 + _dot(gated, wout_ref[cols, :])
    _ffn_stage(x1, out_ref, nffn_ref[...], wg_ref, wu_ref, wd_ref, act_ref,
               final_gain=nfin_ref[...] if final else None)


def _gla_out_ffn(x2, o, r, hgain, wout, nffn, nfin, ffn, layer, *, final, tm):
    tokens, d = x2.shape
    d_ff = ffn[2].shape[1]
    tile = lambda n: pl.BlockSpec((tm, n), lambda i: (i, 0))
    return pl.pallas_call(
        functools.partial(_gla_out_ffn_kernel, final=final),
        grid=(tokens // tm,),
        in_specs=[tile(d), pl.BlockSpec((o.shape[0], tm, o.shape[2]), lambda i: (0, i, 0)),
                  tile(r.shape[1])]
        + [_const_spec(a.shape) for a in (hgain, wout, nffn, nfin)]
        + [_layer_spec(w, layer) for w in ffn],
        out_specs=tile(d),
        out_shape=jax.ShapeDtypeStruct((tokens, d), F32),
        scratch_shapes=[pltpu.VMEM((tm, d_ff), BF16)],
        compiler_params=pltpu.CompilerParams(
            dimension_semantics=("arbitrary",), vmem_limit_bytes=VMEM_LIMIT_BYTES),
        name="gla_out_ffn",
    )(x2, o, r, hgain, wout, nffn, nfin, *ffn)


def kernel(x, norm_mix, norm_ffn, norm_final, w_pool, pool_scale, w_gla_in, w_gate_up, b_gate,
           gla_head_norm, w_gla_out, w_ffn_gate, w_ffn_up, w_ffn_down):
    batch, seq_len, d = x.shape
    depth = norm_mix.shape[0]
    key_dim = w_gate_up.shape[-1]
    val_dim = w_gla_out.shape[1]
    dk = key_dim // GLA_HEADS
    assert depth % 2 == 0, "the final norm is fused into the last GLA layer"
    assert seq_len % TOKEN_TILE == 0 and TOKEN_TILE % SCAN_CHUNK == 0
    assert d % (len(POOL_WINDOWS) * LANES) == 0 and dk % LANES == 0
    row = lambda a: a.reshape(1, -1)

    x2 = x.reshape(batch * seq_len, d)
    ffn = (w_ffn_gate.astype(BF16), w_ffn_up.astype(BF16), w_ffn_down.astype(BF16))
    for i in range(depth):
        j = i // 2
        if i % 2 == 0:
            x2 = _pool_ffn(x2, row(norm_mix[i]), w_pool[j].astype(BF16), row(pool_scale[j]),
                           row(norm_ffn[i]), ffn, i, seq_len=seq_len, tm=TOKEN_TILE)
        else:
            w_in = w_gla_in[j]
            i1 = key_dim
            i2 = i1 + key_dim
            i3 = i2 + val_dim
            wrg = jnp.zeros((d, val_dim + LANES), F32).at[:, :val_dim + 2 * GATE_RANK].set(w_in[:, i3:])
            wup = jnp.zeros((LANES, 2 * key_dim), F32)
            wup = wup.at[:GATE_RANK, :key_dim].set(w_gate_up[j, 0])
            wup = wup.at[GATE_RANK:2 * GATE_RANK, key_dim:].set(w_gate_up[j, 1])
            qf, kf, qb, kb, v, r, cv = _gla_in(
                x2, row(norm_mix[i]), w_in[:, :i1].astype(BF16), w_in[:, i1:i2].astype(BF16),
                w_in[:, i2:i3].astype(BF16), wrg.astype(BF16),
                wup.astype(BF16), b_gate[j].reshape(1, -1), q_scale=dk ** -0.5, tm=TOKEN_TILE)
            o = _gla_scan(qf, kf, qb, kb, v, cv, batch=batch, seq_len=seq_len)
            x2 = _gla_out_ffn(x2, o, r, row(gla_head_norm[j]), w_gla_out[j].astype(BF16),
                              row(norm_ffn[i]), row(norm_final), ffn, i,
                              final=(i == depth - 1), tm=TOKEN_TILE)
    return x2.reshape(batch, seq_len, d)
```

```python
import functools

import jax
import jax.numpy as jnp
from jax import lax
from jax.experimental import pallas as pl
from jax.experimental.pallas import tpu as pltpu

F32 = jnp.float32
BF16 = jnp.bfloat16

EPS = 1e-6
POOL_WINDOWS = (2, 4, 8, 16)
GLA_HEADS = 4
GATE_RANK = 16
GATE_NORMALIZER = 16.0

LANES = 128
SUBLANES = 8
MXU_DIM = 256
VMEM_LIMIT_BYTES = 56 * 1024 * 1024

POOL_HALO = SUBLANES
FF_CHUNK = MXU_DIM
SCAN_CHUNK = 128
SCAN_HEADS_PER_STEP = 2
TOKEN_TILE = 512
GLA_IN_TILE = 1024


def _rms(x, gain):
    ms = jnp.mean(x * x, axis=-1, keepdims=True)
    return x * lax.rsqrt(ms + EPS) * gain


def _silu(x):
    return x * jax.nn.sigmoid(x)


def _dot(a, b):
    return jnp.dot(a, b, preferred_element_type=F32)


def _dot_nt(a, b):
    return lax.dot_general(a, b, (((1,), (1,)), ((), ())), preferred_element_type=F32)


def _dot_tn(a, b):
    return lax.dot_general(a, b, (((0,), (0,)), ((), ())), preferred_element_type=F32)


def _ffn_stage(x1, o_ref, gain, wg_ref, wu_ref, wd_ref, act_ref, final_gain=None, side_work=(),
               h_ref=None, after_gate_up=None):
    park_residual = bool(side_work) or after_gate_up is not None
    if park_residual:
        o_ref[...] = x1
    if h_ref is None:
        h = _rms(x1, gain).astype(BF16)
        read_h = lambda: h
    else:
        read_h = lambda: h_ref[...]
    d_ff = wg_ref.shape[1]
    chunks = [slice(s, min(s + FF_CHUNK, d_ff)) for s in range(0, d_ff, FF_CHUNK)]
    assert len(side_work) <= len(chunks)
    work_after = {(j * len(chunks)) // len(side_work): w for j, w in enumerate(side_work)}
    for c, cols in enumerate(chunks):
        g = _dot(read_h(), wg_ref[:, cols])
        u = _dot(read_h(), wu_ref[:, cols])
        act_ref[:, cols] = (_silu(g) * u).astype(BF16)
        if c in work_after:
            work_after[c]()
    if after_gate_up is not None:
        after_gate_up()
    y = (o_ref[...] if park_residual else x1) + _dot(act_ref[...], wd_ref[...])
    o_ref[...] = y if final_gain is None else _rms(y, final_gain)


def _zero_at_first_step(*refs):
    @pl.when(pl.program_id(0) == 0)
    def _():
        for ref in refs:
            ref[...] = jnp.zeros_like(ref)


def _prepared_tile_spec(tm, n, n_tiles):
    return pl.BlockSpec((tm, n), lambda i: (jnp.minimum(i, n_tiles - 1), 0))


def _finished_tile_spec(tm, n):
    return pl.BlockSpec((tm, n), lambda i: (jnp.maximum(i - 1, 0), 0))


def _window_sum(s, win, tm):
    n = s.shape[0]
    assert win & (win - 1) == 0 and win <= 2 * POOL_HALO

    def ahead(a, k):
        return a if k == 0 else pltpu.roll(a, n - k, axis=0)

    width = 1
    while 2 * width < win:
        s = s + ahead(s, width)
        width *= 2
    start = POOL_HALO - win // 2
    return (ahead(s, start) + ahead(s, start + width))[:tm]


def _pool_ffn_kernel(x_ref, xprev_ref, xnext_ref, nmix_ref, wpool_ref, pscale_ref, nffn_ref,
                     wg_ref, wu_ref, wd_ref, o_ref, hx_ref, x1_ref, h1_ref, act_ref,
                     *, seq_len, n_tiles):
    _zero_at_first_step(x1_ref, h1_ref)
    tm = x_ref.shape[0]
    tiles_per_seq = seq_len // tm
    pos = jnp.minimum(pl.program_id(0), n_tiles - 1) % tiles_per_seq
    group = x_ref.shape[1] // len(POOL_WINDOWS)

    def normalise():
        gain = nmix_ref[...]
        hx_ref[POOL_HALO:POOL_HALO + tm, :] = _rms(x_ref[...], gain)
        hx_ref[0:POOL_HALO, :] = jnp.where(pos > 0, _rms(xprev_ref[...], gain), 0.0)
        hx_ref[POOL_HALO + tm:, :] = jnp.where(
            pos < tiles_per_seq - 1, _rms(xnext_ref[...], gain), 0.0)

    def mix_group(g, win):
        left = win // 2
        right = win - 1 - left
        cols = slice(g * group, (g + 1) * group)
        t = pos * tm + lax.broadcasted_iota(jnp.int32, (tm, 1), 0)
        acc = _window_sum(hx_ref[:, cols], win, tm)
        cnt = (jnp.minimum(t + right + 1, seq_len) - jnp.maximum(t - left, 0)).astype(F32)
        mixed = (acc / cnt - hx_ref[POOL_HALO:POOL_HALO + tm, cols]).astype(BF16)
        x1_ref[:, cols] = x_ref[:, cols] + _dot(mixed, wpool_ref[g]) * pscale_ref[:, cols]

    prepare = [normalise] + [functools.partial(mix_group, g, win)
                             for g, win in enumerate(POOL_WINDOWS)]
    def normalise_prepared():
        h1_ref[...] = _rms(x1_ref[...], nffn_ref[...]).astype(BF16)

    _ffn_stage(x1_ref[...], o_ref, nffn_ref[...], wg_ref, wu_ref, wd_ref, act_ref,
               side_work=prepare, h_ref=h1_ref, after_gate_up=normalise_prepared)


def _const_spec(shape):
    nd = len(shape)
    return pl.BlockSpec(shape, lambda *_: (0,) * nd, pipeline_mode=pl.Buffered(1))


def _layer_spec(stacked, layer):
    _, rows, cols = stacked.shape
    return pl.BlockSpec((None, rows, cols), lambda *_: (layer, 0, 0), pipeline_mode=pl.Buffered(1))


def _pool_ffn(x2, nmix, wpool, pscale, nffn, ffn, layer, *, seq_len, tm):
    tokens, d = x2.shape
    n_tiles = tokens // tm
    halo_blocks = tokens // POOL_HALO
    per_tile = tm // POOL_HALO
    d_ff = ffn[2].shape[1]
    tile_of = lambda i: jnp.minimum(i, n_tiles - 1)
    return pl.pallas_call(
        functools.partial(_pool_ffn_kernel, seq_len=seq_len, n_tiles=n_tiles),
        grid=(n_tiles + 1,),
        in_specs=[
            _prepared_tile_spec(tm, d, n_tiles),
            pl.BlockSpec((POOL_HALO, d), lambda i: (jnp.maximum(tile_of(i) * per_tile - 1, 0), 0)),
            pl.BlockSpec((POOL_HALO, d),
                         lambda i: (jnp.minimum((tile_of(i) + 1) * per_tile, halo_blocks - 1), 0)),
            _const_spec(nmix.shape),
            _const_spec(wpool.shape),
            _const_spec(pscale.shape),
            _const_spec(nffn.shape),
        ] + [_layer_spec(w, layer) for w in ffn],
        out_specs=_finished_tile_spec(tm, d),
        out_shape=jax.ShapeDtypeStruct((tokens, d), F32),
        scratch_shapes=[
            pltpu.VMEM((tm + 2 * POOL_HALO, d), F32),
            pltpu.VMEM((tm, d), F32),
            pltpu.VMEM((tm, d), BF16),
            pltpu.VMEM((tm, d_ff), BF16),
        ],
        compiler_params=pltpu.CompilerParams(
            dimension_semantics=("arbitrary",), vmem_limit_bytes=VMEM_LIMIT_BYTES),
        name="pool_ffn",
    )(x2, x2, x2, nmix, wpool, pscale, nffn, *ffn)


def _centred_sum_matrices(c, scale):
    r = lax.broadcasted_iota(jnp.int32, (c, 2 * c), 0)
    j = lax.broadcasted_iota(jnp.int32, (c, 2 * c), 1) & (c - 1)
    one = lambda m: jnp.where(m, scale, 0.0)
    fwd = one(j <= r) - one(j < c // 2)
    bwd = one(j >= r) - one(j >= c // 2)
    return fwd.astype(BF16), bwd.astype(BF16)


def _split_bf16(x):
    hi = x.astype(BF16)
    lo = (x - hi.astype(F32)).astype(BF16)
    return jnp.concatenate([hi, lo], axis=0)


def _log_sigmoid(z):
    return jnp.minimum(z, 0.0) - jnp.log(1.0 + jnp.exp(-jnp.abs(z)))


def _gla_in_kernel(x_ref, nmix_ref, wq_ref, wk_ref, wv_ref, wrg_ref, wup_ref, bg_ref,
                   qf_ref, kf_ref, qb_ref, kb_ref, v_ref, r_ref, cv_ref, q_ref, k_ref, ls_ref,
                   *, q_scale):
    _zero_at_first_step(q_ref, k_ref, ls_ref)
    tm = x_ref.shape[0]
    dk = wq_ref.shape[1]
    dv = wv_ref.shape[1]
    c = SCAN_CHUNK
    n_decay = tm // c
    h = _rms(x_ref[...], nmix_ref[...]).astype(BF16)
    gate_lr = []

    def project_v(cols):
        v_ref[:, cols] = _dot(h, wv_ref[:, cols]).astype(BF16)

    def project_r(cols):
        r_ref[:, cols] = _dot(h, wrg_ref[:, cols])

    def project_r_and_gate(cols):
        rg = _dot(h, wrg_ref[:, cols.start:])
        r_ref[:, cols] = rg[:, :cols.stop - cols.start]
        gate_lr.append(rg[:, cols.stop - cols.start:].astype(BF16))

    assert n_decay % 2 == 0
    piece = 2 * dv // n_decay
    pieces = [slice(j * piece, (j + 1) * piece) for j in range(n_decay // 2)]
    output_projections = (
        [functools.partial(project_v, cols) for cols in pieces]
        + [functools.partial(project_r, cols) for cols in pieces[:-1]]
        + [functools.partial(project_r_and_gate, pieces[-1])])

    sum_fwd, sum_bwd = _centred_sum_matrices(c, 1.0 / GATE_NORMALIZER)
    vec_row = lax.broadcasted_iota(jnp.int32, (SUBLANES, dk), 0)
    for ci in range(n_decay):
        output_projections[ci]()
        rows = slice(ci * c, (ci + 1) * c)
        qc = q_ref[rows, :]
        kc = k_ref[rows, :]
        ls_f = ls_ref[rows, :dk]
        cf = _dot(sum_fwd, _split_bf16(ls_f))
        b_mid = ls_f[0:1] / GATE_NORMALIZER - cf[0:1]
        qf_ref[rows, :] = (qc * jnp.exp(cf)).astype(BF16)
        kf_ref[rows, :] = (kc * jnp.exp(-cf)).astype(BF16)
        ls_b = ls_ref[rows, dk:]
        cb = _dot(sum_bwd, _split_bf16(ls_b))
        s_mid = ls_b[c - 1:c] / GATE_NORMALIZER - cb[c - 1:c]
        qb_ref[rows, :] = (qc * jnp.exp(cb)).astype(BF16)
        kb_ref[rows, :] = (kc * jnp.exp(-cb)).astype(BF16)
        vecs = (jnp.exp(b_mid), jnp.exp(cf[c - 1:c]), jnp.exp(b_mid + cf[c - 1:c]),
                jnp.exp(s_mid), jnp.exp(cb[0:1]), jnp.exp(s_mid + cb[0:1]))
        cv = jnp.zeros((SUBLANES, dk), F32)
        for j, vec in enumerate(vecs):
            cv = jnp.where(vec_row == j, vec, cv)
        cv_ref[ci] = cv

    ls_ref[...] = _log_sigmoid(_dot(gate_lr[0], wup_ref[...]) + bg_ref[...])
    q_ref[...] = _dot(h, wq_ref[...]) * q_scale
    k_ref[...] = _dot(h, wk_ref[...])


def _gla_in(x2, nmix, wq, wk, wv, wrg, wup, bg, *, q_scale, tm):
    tokens, d = x2.shape
    dk = wq.shape[1]
    dv = wv.shape[1]
    n_chunks = tokens // SCAN_CHUNK
    n_tiles = tokens // tm
    decayed = _finished_tile_spec(tm, dk)
    projected = _prepared_tile_spec(tm, dv, n_tiles)
    return pl.pallas_call(
        functools.partial(_gla_in_kernel, q_scale=q_scale),
        grid=(n_tiles + 1,),
        in_specs=[_prepared_tile_spec(tm, d, n_tiles)]
        + [_const_spec(a.shape) for a in (nmix, wq, wk, wv, wrg, wup, bg)],
        out_specs=[decayed, decayed, decayed, decayed, projected, projected,
                   pl.BlockSpec((tm // SCAN_CHUNK, SUBLANES, dk),
                                lambda i: (jnp.maximum(i - 1, 0), 0, 0))],
        out_shape=[jax.ShapeDtypeStruct((tokens, dk), BF16)] * 4
        + [jax.ShapeDtypeStruct((tokens, dv), BF16), jax.ShapeDtypeStruct((tokens, dv), F32),
           jax.ShapeDtypeStruct((n_chunks, SUBLANES, dk), F32)],
        scratch_shapes=[pltpu.VMEM((tm, dk), F32), pltpu.VMEM((tm, dk), F32),
                        pltpu.VMEM((tm, 2 * dk), F32)],
        compiler_params=pltpu.CompilerParams(
            dimension_semantics=("arbitrary",), vmem_limit_bytes=VMEM_LIMIT_BYTES),
        name="gla_in",
    )(x2, nmix, wq, wk, wv, wrg, wup, bg)


def _gla_scan_kernel(qf_ref, kf_ref, qb_ref, kb_ref, v_ref, cv_ref, o_ref, ut_ref, st_ref,
                     *, dk, dv):
    for g in range(qf_ref.shape[1] // dk):
        kc = slice(g * dk, (g + 1) * dk)
        vc = slice(g * dv, (g + 1) * dv)
        _scan_head(qf_ref.at[:, kc], kf_ref.at[:, kc], qb_ref.at[:, kc], kb_ref.at[:, kc],
                   v_ref.at[:, vc], cv_ref.at[:, :, kc], o_ref.at[:, vc], ut_ref.at[g], st_ref.at[g])


def _scan_head(qf_ref, kf_ref, qb_ref, kb_ref, v_ref, cv_ref, o_ref, ut_ref, st_ref):
    c = SCAN_CHUNK
    n = qf_ref.shape[0] // c
    dk = qf_ref.shape[1]

    def both(cv, row):
        return jnp.concatenate([cv[row:row + 1], cv[row + 3:row + 4]], axis=1)

    def increment(i):
        rows = slice(i * c, (i + 1) * c)
        kk = jnp.concatenate([kf_ref[rows, :], kb_ref[rows, :]], axis=1)
        ut_ref[i] = _dot_tn(v_ref[rows, :], kk) * both(cv_ref[i], 1)

    def step_fwd(j):
        st_ref[j + 1, :, :dk] = st_ref[j, :, :dk] * cv_ref[j][2:3] + ut_ref[j, :, :dk]

    def step_bwd(j):
        st_ref[j - 1, :, dk:] = st_ref[j, :, dk:] * cv_ref[j][5:6] + ut_ref[j, :, dk:]

    assert n % 2 == 0
    st_ref[0, :, :dk] = jnp.zeros((st_ref.shape[1], dk), F32)
    st_ref[n - 1, :, dk:] = jnp.zeros((st_ref.shape[1], dk), F32)
    for t in range(n // 2):
        increment(t)
        increment(n - 1 - t)
        if t > 0:
            step_fwd(t - 1)
            step_bwd(n - t)
    for t in range(n // 2, n):
        step_fwd(t - 1)
        step_bwd(n - t)

    qi = lax.broadcasted_iota(jnp.int32, (c, c), 0)
    ki = lax.broadcasted_iota(jnp.int32, (c, c), 1)
    fwd_mask = ki <= qi
    no_keys = jnp.zeros((c, dk), BF16)

    def score(i):
        rows = slice(i * c, (i + 1) * c)
        qq = jnp.concatenate([qf_ref[rows, :], qb_ref[rows, :]], axis=1)
        keys = jnp.concatenate([jnp.concatenate([kf_ref[rows, :], no_keys], axis=1),
                                jnp.concatenate([no_keys, kb_ref[rows, :]], axis=1)], axis=0)
        return qq, _dot_nt(qq, keys)

    upcoming = score(0)
    for i in range(n):
        rows = slice(i * c, (i + 1) * c)
        qq, scores = upcoming
        if i + 1 < n:
            upcoming = score(i + 1)
        p = jnp.where(fwd_mask, scores[:, :c], scores[:, c:]).astype(BF16)
        seen = (st_ref[i] * both(cv_ref[i], 0)).astype(BF16)
        o_ref[rows, :] = _dot(p, v_ref[rows, :]) + _dot_nt(qq, seen)


def _gla_scan(qf, kf, qb, kb, v, cv, *, batch, seq_len):
    tokens, dk_all = qf.shape
    dv_all = v.shape[1]
    dk = dk_all // GLA_HEADS
    dv = dv_all // GLA_HEADS
    n = seq_len // SCAN_CHUNK
    g = SCAN_HEADS_PER_STEP
    assert GLA_HEADS % g == 0
    qk_spec = pl.BlockSpec((seq_len, g * dk), lambda b, h: (b, h))
    return pl.pallas_call(
        functools.partial(_gla_scan_kernel, dk=dk, dv=dv),
        grid=(batch, GLA_HEADS // g),
        in_specs=[qk_spec, qk_spec, qk_spec, qk_spec,
                  pl.BlockSpec((seq_len, g * dv), lambda b, h: (b, h)),
                  pl.BlockSpec((n, SUBLANES, g * dk), lambda b, h: (b, 0, h))],
        out_specs=pl.BlockSpec((seq_len, g * dv), lambda b, h: (b, h)),
        out_shape=jax.ShapeDtypeStruct((tokens, dv_all), F32),
        scratch_shapes=[pltpu.VMEM((g, n, dv, 2 * dk), F32), pltpu.VMEM((g, n, dv, 2 * dk), F32)],
        compiler_params=pltpu.CompilerParams(
            dimension_semantics=("arbitrary", "arbitrary"), vmem_limit_bytes=VMEM_LIMIT_BYTES),
        name="gla_scan",
    )(qf, kf, qb, kb, v, cv)


def _gla_out_ffn_kernel(x_ref, o_ref, r_ref, hgain_ref, wout_ref, nffn_ref, nfin_ref,
                        wg_ref, wu_ref, wd_ref, out_ref, act_ref, *, final):
    dv = hgain_ref.shape[1]
    x1 = x_ref[...]
    for hh in range(o_ref.shape[1] // dv):
        cols = slice(hh * dv, (hh + 1) * dv)
        gated = (_rms(o_ref[:, cols], hgain_ref[...]) * _silu(r_ref[:, cols])).astype(BF16)
        x1 = x1 + _dot(gated, wout_ref[cols, :])
    _ffn_stage(x1, out_ref, nffn_ref[...], wg_ref, wu_ref, wd_ref, act_ref,
               final_gain=nfin_ref[...] if final else None)


def _gla_out_ffn(x2, o, r, hgain, wout, nffn, nfin, ffn, layer, *, final, tm):
    tokens, d = x2.shape
    d_ff = ffn[2].shape[1]
    tile = lambda n: pl.BlockSpec((tm, n), lambda i: (i, 0))
    return pl.pallas_call(
        functools.partial(_gla_out_ffn_kernel, final=final),
        grid=(tokens // tm,),
        in_specs=[tile(d), tile(o.shape[1]), tile(r.shape[1])]
        + [_const_spec(a.shape) for a in (hgain, wout, nffn, nfin)]
        + [_layer_spec(w, layer) for w in ffn],
        out_specs=tile(d),
        out_shape=jax.ShapeDtypeStruct((tokens, d), F32),
        scratch_shapes=[pltpu.VMEM((tm, d_ff), BF16)],
        compiler_params=pltpu.CompilerParams(
            dimension_semantics=("arbitrary",), vmem_limit_bytes=VMEM_LIMIT_BYTES),
        name="gla_out_ffn",
    )(x2, o, r, hgain, wout, nffn, nfin, *ffn)


def kernel(x, norm_mix, norm_ffn, norm_final, w_pool, pool_scale, w_gla_in, w_gate_up, b_gate,
           gla_head_norm, w_gla_out, w_ffn_gate, w_ffn_up, w_ffn_down):
    batch, seq_len, d = x.shape
    depth = norm_mix.shape[0]
    key_dim = w_gate_up.shape[-1]
    val_dim = w_gla_out.shape[1]
    dk = key_dim // GLA_HEADS
    assert depth % 2 == 0, "the final norm is fused into the last GLA layer"
    assert seq_len % TOKEN_TILE == 0 and seq_len % GLA_IN_TILE == 0 and GLA_IN_TILE % SCAN_CHUNK == 0
    assert d % (len(POOL_WINDOWS) * LANES) == 0 and dk % LANES == 0
    row = lambda a: a.reshape(1, -1)

    x2 = x.reshape(batch * seq_len, d)
    ffn = (w_ffn_gate.astype(BF16), w_ffn_up.astype(BF16), w_ffn_down.astype(BF16))
    for i in range(depth):
        j = i // 2
        if i % 2 == 0:
            x2 = _pool_ffn(x2, row(norm_mix[i]), w_pool[j].astype(BF16), row(pool_scale[j]),
                           row(norm_ffn[i]), ffn, i, seq_len=seq_len, tm=TOKEN_TILE)
        else:
            w_in = w_gla_in[j]
            i1 = key_dim
            i2 = i1 + key_dim
            i3 = i2 + val_dim
            wrg = jnp.zeros((d, val_dim + LANES), F32).at[:, :val_dim + 2 * GATE_RANK].set(w_in[:, i3:])
            wup = jnp.zeros((LANES, 2 * key_dim), F32)
            wup = wup.at[:GATE_RANK, :key_dim].set(w_gate_up[j, 0])
            wup = wup.at[GATE_RANK:2 * GATE_RANK, key_dim:].set(w_gate_up[j, 1])
            qf, kf, qb, kb, v, r, cv = _gla_in(
                x2, row(norm_mix[i]), w_in[:, :i1].astype(BF16), w_in[:, i1:i2].astype(BF16),
                w_in[:, i2:i3].astype(BF16), wrg.astype(BF16),
                wup.astype(BF16), b_gate[j].reshape(1, -1), q_scale=dk ** -0.5, tm=GLA_IN_TILE)
            o = _gla_scan(qf, kf, qb, kb, v, cv, batch=batch, seq_len=seq_len)
            x2 = _gla_out_ffn(x2, o, r, row(gla_head_norm[j]), w_gla_out[j].astype(BF16),
                              row(norm_ffn[i]), row(norm_final), ffn, i,
                              final=(i == depth - 1), tm=TOKEN_TILE)
    return x2.reshape(batch, seq_len, d)
```
